```python
import math
import jax, jax.numpy as jnp
from jax import lax
import numpy as np

D_MODEL = 2048
BATCH = 2
SEQ = 4096
DEPTH = 4
DEC_BATCH = 8
DEC_SEQ = 8
PAST_LEN = 16384
PAGE_SIZE = 128

HEAD_DIM = 128
N_HEADS = D_MODEL // HEAD_DIM
DIL_GROUPS = ((128, 1), (512, 4), (2048, 16))
N_GROUPS = len(DIL_GROUPS)
DIFF_HALF = HEAD_DIM // 2
MOBA_BLOCK = 256
MOBA_TOPK = 3
MOBA_Q_CHUNK = 32
DENSE_Q_BLOCK = 128
D_FF = 4 * D_MODEL
ROPE_THETA = 10000.0
NORM_EPS = 1e-6
SUBLN_EPS = 1e-5
N_MIXERS = 3
N_A = (DEPTH + 2) // N_MIXERS
N_B = (DEPTH + 1) // N_MIXERS
N_C = DEPTH // N_MIXERS
N_PAGED = N_B + N_C

kernel_name = 'hybrid_dilated_diff_moba_step'


def rms_norm(x, g, eps=NORM_EPS):
    x32 = x.astype(jnp.float32)
    y = x32 * lax.rsqrt(jnp.mean(x32 * x32, axis=-1, keepdims=True) + eps)
    return (y * g.astype(jnp.float32)).astype(x.dtype)


def rope(x, pos):
    dh = x.shape[-1]
    inv = ROPE_THETA ** (-jnp.arange(0, dh, 2, dtype=jnp.float32) / dh)
    ang = pos.astype(jnp.float32)[:, None] * inv[None, :]
    ang = ang.reshape((1, pos.shape[0]) + (1,) * (x.ndim - 3) + (dh // 2,))
    cos, sin = jnp.cos(ang), jnp.sin(ang)
    x32 = x.astype(jnp.float32)
    x1, x2 = x32[..., : dh // 2], x32[..., dh // 2:]
    return jnp.concatenate([x1 * cos - x2 * sin, x2 * cos + x1 * sin], axis=-1).astype(x.dtype)


def masked_softmax(s):
    m = jnp.max(s, axis=-1, keepdims=True)
    e = jnp.exp(s - m)
    den = jnp.sum(e, axis=-1, keepdims=True)
    return e / den, (m + jnp.log(den))[..., 0]


def modulation(c, w_mod, b_mod):
    m = jax.nn.silu(c) @ w_mod + b_mod
    return jnp.split(m[:, None, :], 6, axis=-1)


def modulate(h, shift, scale):
    return h * (1.0 + scale) + shift


def strided(x, dil):
    b, S = x.shape[:2]
    rest = x.shape[2:]
    return jnp.swapaxes(x.reshape((b, S // dil, dil) + rest), 1, 2).reshape((b * dil, S // dil) + rest)


def unstrided(x, b, dil):
    L = x.shape[1]
    rest = x.shape[2:]
    return jnp.swapaxes(x.reshape((b, dil, L) + rest), 1, 2).reshape((b, L * dil) + rest)


def dilated_window_prompt(q, k, v, window, dil):
    b, S, H, hd = q.shape
    nw = window // dil
    L = S // dil
    Lp = -(-L // nw) * nw
    nq = Lp // nw
    N = b * dil

    def blocks(t):
        t = jnp.pad(strided(t, dil), ((0, 0), (0, Lp - L), (0, 0), (0, 0)))
        return t.reshape(N, nq, nw, H, hd)

    def band(tb):
        prev = jnp.pad(tb, ((0, 0), (1, 0), (0, 0), (0, 0), (0, 0)))[:, :-1]
        return jnp.concatenate([prev, tb], axis=2)

    qb = blocks(q)
    kb = band(blocks(k))
    vb = band(blocks(v))
    s = jnp.einsum('nbqhd,nbkhd->nbhqk', qb, kb, preferred_element_type=jnp.float32) * (hd ** -0.5)
    qi = jnp.arange(nw)[:, None]
    ki = jnp.arange(2 * nw)[None, :]
    blk = jnp.arange(nq)[:, None, None]
    dist = nw + qi - ki
    valid = (dist >= 0) & (dist <= nw) & (blk * nw + ki >= nw)
    p, lse = masked_softmax(jnp.where(valid[None, :, None], s, -jnp.inf))
    o = jnp.einsum('nbhqk,nbkhd->nbqhd', p, vb.astype(jnp.float32))
    o = unstrided(o.reshape(N, Lp, H, hd)[:, :L], b, dil)
    lse = unstrided(jnp.swapaxes(lse, 2, 3).reshape(N, Lp, H)[:, :L], b, dil)
    return o, lse


def dilated_window_sample(q, k_ctx, v_ctx, n_buf, window, dil):
    b, n, H, hd = q.shape
    nw = window // dil
    idx = n_buf + jnp.arange(n)[:, None] - dil * jnp.arange(nw + 1)[None, :]
    valid = idx >= 0
    idx = jnp.maximum(idx, 0)
    kg = k_ctx[:, idx]
    vg = v_ctx[:, idx]
    s = jnp.einsum('bqhd,bqmhd->bhqm', q, kg, preferred_element_type=jnp.float32) * (hd ** -0.5)
    p, lse = masked_softmax(jnp.where(valid[None, None], s, -jnp.inf))
    o = jnp.einsum('bhqm,bqmhd->bqhd', p, vg.astype(jnp.float32))
    return o, jnp.swapaxes(lse, 1, 2)


def mixer_dilated(h, pos, w_qkv, w_o, bufs=None):
    b, L, _ = h.shape
    qkv = (h @ w_qkv).reshape(b, L, N_GROUPS, 3, N_HEADS, HEAD_DIM)
    outs, lses, states = [], [], []
    for g, (window, dil) in enumerate(DIL_GROUPS):
        q = rope(qkv[:, :, g, 0], pos)
        k = rope(qkv[:, :, g, 1], pos)
        v = qkv[:, :, g, 2]
        if bufs is None:
            o, lse = dilated_window_prompt(q, k, v, window, dil)
            keep = min(window, L)
            states.append(jnp.stack([k[:, L - keep:], v[:, L - keep:]], axis=2))
        else:
            n_buf = bufs[g].shape[1]
            k_ctx = jnp.concatenate([bufs[g][:, :, 0], k], axis=1)
            v_ctx = jnp.concatenate([bufs[g][:, :, 1], v], axis=1)
            o, lse = dilated_window_sample(q, k_ctx, v_ctx, n_buf, window, dil)
            states.append(jnp.stack([k_ctx[:, L:], v_ctx[:, L:]], axis=2))
        outs.append(o)
        lses.append(lse)
    wts = jax.nn.softmax(jnp.stack(lses), axis=0)
    o = jnp.sum(wts[..., None] * jnp.stack(outs), axis=0)
    return o.astype(h.dtype).reshape(b, L, N_HEADS * HEAD_DIM) @ w_o, states


def diff_lambda(lam, layer_idx):
    lam_init = 0.8 - 0.6 * math.exp(-0.3 * layer_idx)
    l32 = lam.astype(jnp.float32)
    lam_full = jnp.exp(jnp.sum(l32[0] * l32[1])) - jnp.exp(jnp.sum(l32[2] * l32[3])) + lam_init
    return lam_full, lam_init


def diff_core(q, segments, lam_full):
    scale = DIFF_HALF ** -0.5
    scores = [jnp.where(mask[None, None, None],
                        jnp.einsum('bqhcd,bkhcd->bchqk', q, k, preferred_element_type=jnp.float32) * scale,
                        -jnp.inf)
              for k, _, mask in segments]
    p = jax.nn.softmax(jnp.concatenate(scores, axis=-1), axis=-1)
    a = p[:, 0] - lam_full * p[:, 1]
    outs = []
    start = 0
    for k, v, _ in segments:
        n = k.shape[1]
        outs.append(jnp.einsum('bhqk,bkhd->bqhd', a[..., start:start + n], v.astype(jnp.float32)))
        start += n
    return sum(outs)


def mixer_diff(h, pos, w_qkv, w_o, lam, g_sub, layer_idx, past=None):
    b, L, _ = h.shape
    qkv = (h @ w_qkv).reshape(b, L, 3, N_HEADS, 2, DIFF_HALF)
    q = rope(qkv[:, :, 0], pos)
    k = rope(qkv[:, :, 1], pos)
    v = qkv[:, :, 2].reshape(b, L, N_HEADS, HEAD_DIM)
    lam_full, lam_init = diff_lambda(lam, layer_idx)
    if past is None:
        nqb = L // DENSE_Q_BLOCK
        qb = jnp.swapaxes(q.reshape((b, nqb, DENSE_Q_BLOCK) + q.shape[2:]), 0, 1)
        kpos = jnp.arange(L)

        def one_block(args):
            qi, start = args
            qpos = start + jnp.arange(DENSE_Q_BLOCK)
            return diff_core(qi, [(k, v, kpos[None, :] <= qpos[:, None])], lam_full)

        o = lax.map(one_block, (qb, jnp.arange(nqb) * DENSE_Q_BLOCK))
        o = jnp.swapaxes(o, 0, 1).reshape(b, L, N_HEADS, HEAD_DIM)
    else:
        k_past, v_past = past
        P = k_past.shape[1]
        k_past = k_past.reshape(b, P, N_HEADS, 2, DIFF_HALF)
        causal = jnp.arange(L)[None, :] <= jnp.arange(L)[:, None]
        o = diff_core(q, [(k_past, v_past, jnp.ones((L, P), bool)), (k, v, causal)], lam_full)
    o = rms_norm(o, g_sub, SUBLN_EPS) * (1.0 - lam_init)
    y = o.astype(h.dtype).reshape(b, L, N_HEADS * HEAD_DIM) @ w_o
    rows = jnp.stack([k.reshape(b, L, N_HEADS, HEAD_DIM), v], axis=2)
    return y, rows


def moba_blocks(k_parts, v_parts):
    b, _, H, hd = k_parts[0].shape
    T = sum(t.shape[1] for t in k_parts)
    Tp = -(-T // MOBA_BLOCK) * MOBA_BLOCK

    def build(parts):
        parts = list(parts)
        if Tp > T:
            parts.append(jnp.zeros((b, Tp - T, H, hd), parts[0].dtype))
        return jnp.concatenate(parts, axis=1).reshape(b, Tp // MOBA_BLOCK, MOBA_BLOCK, H, hd)

    kb = build(k_parts)
    vb = build(v_parts)
    k_mean = jnp.mean(kb, axis=2, dtype=jnp.float32)
    return kb, vb, k_mean


def moba_core(q, q_pos, kb, vb, k_mean):
    b, Lq, H, hd = q.shape
    nb = kb.shape[1]
    own = q_pos // MOBA_BLOCK
    gate = jnp.einsum('bqhd,bnhd->bhqn', q.astype(jnp.float32), k_mean)
    past = jnp.arange(nb)[None, :] < own[:, None]
    gate = jnp.where(past[None, None], gate, -jnp.inf)
    n_sel = min(MOBA_TOPK, nb)
    _, sel = lax.top_k(gate, n_sel)
    sel_ok = sel < own[None, None, :, None]
    bi = jnp.arange(b)[:, None, None, None]
    hi = jnp.arange(H)[None, :, None, None]
    k_sel = kb[bi, sel, :, hi]
    v_sel = vb[bi, sel, :, hi]
    own_b = own[None, None, :]
    k_own = kb[bi[..., 0], own_b, :, hi[..., 0]]
    v_own = vb[bi[..., 0], own_b, :, hi[..., 0]]
    scale = hd ** -0.5
    s_sel = jnp.einsum('bqhd,bhqskd->bhqsk', q, k_sel, preferred_element_type=jnp.float32) * scale
    s_sel = jnp.where(sel_ok[..., None], s_sel, -jnp.inf).reshape(b, H, Lq, n_sel * MOBA_BLOCK)
    key_pos = own[:, None] * MOBA_BLOCK + jnp.arange(MOBA_BLOCK)[None, :]
    s_own = jnp.einsum('bqhd,bhqkd->bhqk', q, k_own, preferred_element_type=jnp.float32) * scale
    s_own = jnp.where((key_pos <= q_pos[:, None])[None, None], s_own, -jnp.inf)
    p = jax.nn.softmax(jnp.concatenate([s_sel, s_own], axis=-1), axis=-1)
    p_sel = p[..., : n_sel * MOBA_BLOCK].reshape(b, H, Lq, n_sel, MOBA_BLOCK)
    p_own = p[..., n_sel * MOBA_BLOCK:]
    return (jnp.einsum('bhqsk,bhqskd->bqhd', p_sel, v_sel.astype(jnp.float32))
            + jnp.einsum('bhqk,bhqkd->bqhd', p_own, v_own.astype(jnp.float32)))


def mixer_moba(h, pos, w_qkv, w_o, past=None):
    b, L, _ = h.shape
    qkv = (h @ w_qkv).reshape(b, L, 3, N_HEADS, HEAD_DIM)
    q = rope(qkv[:, :, 0], pos)
    k = rope(qkv[:, :, 1], pos)
    v = qkv[:, :, 2]
    if past is None:
        kb, vb, km = moba_blocks([k], [v])
        nqc = L // MOBA_Q_CHUNK
        qc = jnp.swapaxes(q.reshape(b, nqc, MOBA_Q_CHUNK, N_HEADS, HEAD_DIM), 0, 1)
        o = lax.map(lambda a: moba_core(a[0], a[1], kb, vb, km), (qc, pos.reshape(nqc, MOBA_Q_CHUNK)))
        o = jnp.swapaxes(o, 0, 1).reshape(b, L, N_HEADS, HEAD_DIM)
    else:
        kb, vb, km = moba_blocks([past[0], k], [past[1], v])
        o = moba_core(q, pos, kb, vb, km)
    y = o.astype(h.dtype).reshape(b, L, N_HEADS * HEAD_DIM) @ w_o
    return y, jnp.stack([k, v], axis=2)


def gather_past(cache_kv, slot, page_table):
    b, n_pages = page_table.shape
    shape = (b, n_pages * PAGE_SIZE, N_HEADS, HEAD_DIM)
    return (cache_kv[slot, page_table, :, 0].reshape(shape),
            cache_kv[slot, page_table, :, 1].reshape(shape))


def sq_relu_mlp(h, w_up, w_down):
    a = jax.nn.relu(h @ w_up)
    return (a * a) @ w_down


def setup_inputs(seed: int = 0) -> dict:
    key = jax.random.key(seed)
    ks = iter(jax.random.split(key, 32))
    f32 = jnp.float32
    D = D_MODEL
    n_pages = PAST_LEN // PAGE_SIZE
    n_used = DEC_BATCH * n_pages
    n_pool = n_used + n_used // 4
    row = (2, N_HEADS, HEAD_DIM)
    qkv_w = N_HEADS * HEAD_DIM

    def nrm(shape, scale):
        return jax.random.normal(next(ks), shape, f32) * scale

    inputs = {}
    inputs['x_prompt'] = nrm((BATCH, SEQ, D), 1.0)
    inputs['x_sample'] = nrm((DEC_BATCH, DEC_SEQ, D), 1.0)
    inputs['c_prompt'] = nrm((BATCH, D), 1.0)
    inputs['c_sample'] = nrm((DEC_BATCH, D), 1.0)
    inputs['cache_win1_kv'] = nrm((N_A, DEC_BATCH, min(DIL_GROUPS[0][0], PAST_LEN)) + row, 1.0)
    inputs['cache_win2_kv'] = nrm((N_A, DEC_BATCH, min(DIL_GROUPS[1][0], PAST_LEN)) + row, 1.0)
    inputs['cache_win3_kv'] = nrm((N_A, DEC_BATCH, min(DIL_GROUPS[2][0], PAST_LEN)) + row, 1.0)
    inputs['cache_kv'] = nrm((N_PAGED, n_pool, PAGE_SIZE) + row, 1.0)
    perm = jax.random.permutation(next(ks), n_pool)
    inputs['page_table'] = perm[:n_used].reshape(DEC_BATCH, n_pages).astype(jnp.int32)
    inputs['norm1_g'] = 1.0 + nrm((DEPTH, D), 0.02)
    inputs['norm2_g'] = 1.0 + nrm((DEPTH, D), 0.02)
    inputs['w_mod'] = nrm((DEPTH, D, 6 * D), 0.5 * D ** -0.5)
    inputs['b_mod'] = nrm((DEPTH, 6 * D), 0.02)
    inputs['w_qkv_a'] = nrm((N_A, D, N_GROUPS * 3 * qkv_w), D ** -0.5)
    inputs['w_o_a'] = nrm((N_A, qkv_w, D), qkv_w ** -0.5)
    inputs['w_qkv_b'] = nrm((N_B, D, 3 * qkv_w), D ** -0.5)
    inputs['w_o_b'] = nrm((N_B, qkv_w, D), qkv_w ** -0.5)
    inputs['lambda_b'] = nrm((N_B, 4, DIFF_HALF), 0.1)
    inputs['subln_g_b'] = 1.0 + nrm((N_B, HEAD_DIM), 0.02)
    inputs['w_qkv_c'] = nrm((N_C, D, 3 * qkv_w), D ** -0.5)
    inputs['w_o_c'] = nrm((N_C, qkv_w, D), qkv_w ** -0.5)
    inputs['w_up'] = nrm((DEPTH, D, D_FF), D ** -0.5)
    inputs['w_down'] = nrm((DEPTH, D_FF, D), D_FF ** -0.5)
    inputs['final_g'] = 1.0 + nrm((D,), 0.02)
    return inputs


def reference(x_prompt, x_sample, c_prompt, c_sample, cache_win1_kv, cache_win2_kv, cache_win3_kv,
              cache_kv, page_table, norm1_g, norm2_g, w_mod, b_mod, w_qkv_a, w_o_a, w_qkv_b, w_o_b,
              lambda_b, subln_g_b, w_qkv_c, w_o_c, w_up, w_down, final_g):
    past_len = page_table.shape[1] * PAGE_SIZE
    pos_p = jnp.arange(x_prompt.shape[1], dtype=jnp.int32)
    pos_s = past_len + jnp.arange(x_sample.shape[1], dtype=jnp.int32)
    win_caches = (cache_win1_kv, cache_win2_kv, cache_win3_kv)
    xp, xs = x_prompt, x_sample
    win_p = [[] for _ in range(N_GROUPS)]
    win_s = [[] for _ in range(N_GROUPS)]
    kv_p, kv_s = [], []
    for i in range(DEPTH):
        kind = i % N_MIXERS
        j = i // N_MIXERS
        mp = modulation(c_prompt, w_mod[i], b_mod[i])
        ms = modulation(c_sample, w_mod[i], b_mod[i])
        hp = modulate(rms_norm(xp, norm1_g[i]), mp[0], mp[1])
        hs = modulate(rms_norm(xs, norm1_g[i]), ms[0], ms[1])
        if kind == 0:
            yp, st_p = mixer_dilated(hp, pos_p, w_qkv_a[j], w_o_a[j])
            ys, st_s = mixer_dilated(hs, pos_s, w_qkv_a[j], w_o_a[j], [cw[j] for cw in win_caches])
            for g in range(N_GROUPS):
                win_p[g].append(st_p[g])
                win_s[g].append(st_s[g])
        else:
            slot = i - (i + 2) // N_MIXERS
            past = gather_past(cache_kv, slot, page_table)
            if kind == 1:
                yp, rp = mixer_diff(hp, pos_p, w_qkv_b[j], w_o_b[j], lambda_b[j], subln_g_b[j], i)
                ys, rs = mixer_diff(hs, pos_s, w_qkv_b[j], w_o_b[j], lambda_b[j], subln_g_b[j], i, past)
            else:
                yp, rp = mixer_moba(hp, pos_p, w_qkv_c[j], w_o_c[j])
                ys, rs = mixer_moba(hs, pos_s, w_qkv_c[j], w_o_c[j], past)
            kv_p.append(rp)
            kv_s.append(rs)
        xp = xp + mp[2] * yp
        xs = xs + ms[2] * ys
        xp = xp + mp[5] * sq_relu_mlp(modulate(rms_norm(xp, norm2_g[i]), mp[3], mp[4]), w_up[i], w_down[i])
        xs = xs + ms[5] * sq_relu_mlp(modulate(rms_norm(xs, norm2_g[i]), ms[3], ms[4]), w_up[i], w_down[i])
    y_prompt = rms_norm(xp, final_g)
    y_sample = rms_norm(xs, final_g)
    return (y_prompt, y_sample,
            jnp.stack(win_p[0]), jnp.stack(win_p[1]), jnp.stack(win_p[2]), jnp.stack(kv_p),
            jnp.stack(win_s[0]), jnp.stack(win_s[1]), jnp.stack(win_s[2]), jnp.stack(kv_s))
```

```python
import functools
import math

import numpy as np
import jax
import jax.numpy as jnp
from jax import lax
from jax.experimental import pallas as pl
from jax.experimental.pallas import tpu as pltpu

F32 = jnp.float32
BF16 = jnp.bfloat16

HEAD_DIM = 128
DIL_GROUPS = ((128, 1), (512, 4), (2048, 16))
N_GROUPS = len(DIL_GROUPS)
DIFF_HALF = HEAD_DIM // 2
MOBA_BLOCK = 256
MOBA_TOPK = 3
PAGE_SIZE = 128
ROPE_THETA = 10000.0
NORM_EPS = 1e-6
SUBLN_EPS = 1e-5
N_MIXERS = 3

LANES = 128
SUBLANES = 8
NEG = -1e30
VMEM_LIMIT = 56 * 1024 * 1024
PAGES_PER_STEP = 4
ROW_CHUNK = 128
WIN_TILE = 2048


def _params(*sem):
    return pltpu.CompilerParams(dimension_semantics=sem, vmem_limit_bytes=VMEM_LIMIT)


def _dot(a, b):
    return jnp.dot(a, b, preferred_element_type=F32)


def _dot_nt(a, b):
    return lax.dot_general(a, b, (((1,), (1,)), ((), ())), preferred_element_type=F32)


def _split_bf16(x):
    hi = x.astype(BF16)
    lo = (x - hi.astype(F32)).astype(BF16)
    return hi, lo


def _dot_nt_3pass(a, b):
    ah, al = _split_bf16(a)
    bh, bl = _split_bf16(b)
    return _dot_nt(ah, bh) + _dot_nt(ah, bl) + _dot_nt(al, bh)


def _norm_mod(x, g, scale, shift):
    y = x * lax.rsqrt(jnp.mean(x * x, axis=-1, keepdims=True) + NORM_EPS) * g
    return y * (1.0 + scale) + shift


def _for_row_chunks(bb, tl, body):
    if bb > 1 or tl <= ROW_CHUNK:
        body(slice(None))
        return

    def step(c, carry):
        body(pl.ds(pl.multiple_of(c * ROW_CHUNK, ROW_CHUNK), ROW_CHUNK))
        return carry

    lax.fori_loop(0, tl // ROW_CHUNK, step, 0)


def _store_norm_mod(x_ref, g_ref, sc_ref, sh_ref, h_s):
    bb, tl, d = x_ref.shape

    def body(rows):
        h = _norm_mod(x_ref[:, rows, :], g_ref[...], sc_ref[...], sh_ref[...])
        h_s[rows, :] = h.reshape(-1, d).astype(BF16)

    _for_row_chunks(bb, tl, body)


def _batch_block(b, l, tl):
    return b if tl == l and b * l <= 1024 else 1


def _mod_kernel(c_ref, w_ref, b_ref, o_ref):
    c = c_ref[...]
    a = (c / (1.0 + jnp.exp(-c))).astype(BF16)
    o_ref[...] = _dot(a, w_ref[...].astype(BF16)) + b_ref[...]


def _modulation(c_all, w_mod, b_mod):
    depth, d, n = w_mod.shape
    rows = c_all.shape[0]
    tn = 1024
    return pl.pallas_call(
        _mod_kernel,
        grid=(depth, n // tn),
        in_specs=[pl.BlockSpec((rows, d), lambda i, j: (0, 0)),
                  pl.BlockSpec((None, d, tn), lambda i, j: (i, 0, j)),
                  pl.BlockSpec((None, 1, tn), lambda i, j: (i, 0, j))],
        out_specs=pl.BlockSpec((None, rows, tn), lambda i, j: (i, 0, j)),
        out_shape=jax.ShapeDtypeStruct((depth, rows, n), F32),
        compiler_params=_params("arbitrary", "arbitrary"),
        name="modulation",
    )(c_all, w_mod, b_mod.reshape(depth, 1, n))


def _rope_chunk(x, cos, sin, half):
    if 2 * half == LANES:
        rot = pltpu.roll(x, half, 1)
    else:
        lane = lax.broadcasted_iota(jnp.int32, x.shape, 1)
        rot = jnp.where(lane % (2 * half) < half, pltpu.roll(x, LANES - half, 1), pltpu.roll(x, half, 1))
    return x * cos + rot * sin


def _qkv_kernel(x_ref, g_ref, sc_ref, sh_ref, cos_ref, sin_ref, w_ref, o_ref, h_s, *, half, tiles_per_kind):
    n = pl.program_id(2)
    bb, tl, d = x_ref.shape
    tn = w_ref.shape[1]

    @pl.when(n == 0)
    def _():
        _store_norm_mod(x_ref, g_ref, sc_ref, sh_ref, h_s)

    acc = _dot(h_s[...], w_ref[...].astype(BF16))
    kind = (n // tiles_per_kind) % 3

    @pl.when(kind < 2)
    def _():
        cos = cos_ref[...]
        sin = sin_ref[...]
        for c in range(tn // LANES):
            chunk = _rope_chunk(acc[:, c * LANES:(c + 1) * LANES], cos, sin, half)
            o_ref[:, c, :, :] = chunk.reshape(bb, tl, LANES)

    @pl.when(kind == 2)
    def _():
        for c in range(tn // LANES):
            o_ref[:, c, :, :] = acc[:, c * LANES:(c + 1) * LANES].reshape(bb, tl, LANES)


def _qkv(x, g, scale, shift, cos, sin, w, *, half, tl):
    b, l, d = x.shape
    n = w.shape[1]
    nh = d // HEAD_DIM
    tn = 512
    hpt = tn // HEAD_DIM
    tpk = d // tn
    bb = _batch_block(b, l, tl)
    return pl.pallas_call(
        functools.partial(_qkv_kernel, half=half, tiles_per_kind=tpk),
        grid=(b // bb, l // tl, n // tn),
        in_specs=[pl.BlockSpec((bb, tl, d), lambda i, j, k: (i, j, 0)),
                  pl.BlockSpec((1, d), lambda i, j, k: (0, 0)),
                  pl.BlockSpec((bb, 1, d), lambda i, j, k: (i, 0, 0)),
                  pl.BlockSpec((bb, 1, d), lambda i, j, k: (i, 0, 0)),
                  pl.BlockSpec((bb * tl, LANES), lambda i, j, k: (j, 0)),
                  pl.BlockSpec((bb * tl, LANES), lambda i, j, k: (j, 0)),
                  pl.BlockSpec((d, tn), lambda i, j, k: (0, k))],
        out_specs=pl.BlockSpec((bb, None, hpt, tl, HEAD_DIM), lambda i, j, k: (i, k // tpk, k % tpk, j, 0)),
        out_shape=jax.ShapeDtypeStruct((b, n // d, nh, l, HEAD_DIM), F32),
        scratch_shapes=[pltpu.VMEM((bb * tl, d), BF16)],
        compiler_params=_params("arbitrary", "arbitrary", "arbitrary"),
        name="qkv",
    )(x, g.reshape(1, d), scale, shift, cos, sin, w)


def _proj_kernel(a_ref, w_ref, gate_ref, res_ref, o_ref, a_s):
    n = pl.program_id(2)
    bb, nh, tl, hd = a_ref.shape
    tn = w_ref.shape[1]

    @pl.when(n == 0)
    def _():
        def body(rows):
            for h in range(nh):
                a_s[rows, h * hd:(h + 1) * hd] = a_ref[:, h, rows, :].reshape(-1, hd).astype(BF16)

        _for_row_chunks(bb, tl, body)

    acc = _dot(a_s[...], w_ref[...].astype(BF16))
    o_ref[...] = res_ref[...] + gate_ref[...] * acc.reshape(bb, tl, tn)


def _proj(a, w, gate, res, *, tl):
    b, l, d = res.shape
    nh = a.shape[1]
    tn = 512
    bb = _batch_block(b, l, tl)
    return pl.pallas_call(
        _proj_kernel,
        grid=(b // bb, l // tl, d // tn),
        in_specs=[pl.BlockSpec((bb, nh, tl, HEAD_DIM), lambda i, j, k: (i, 0, j, 0)),
                  pl.BlockSpec((d, tn), lambda i, j, k: (0, k)),
                  pl.BlockSpec((bb, 1, tn), lambda i, j, k: (i, 0, k)),
                  pl.BlockSpec((bb, tl, tn), lambda i, j, k: (i, j, k))],
        out_specs=pl.BlockSpec((bb, tl, tn), lambda i, j, k: (i, j, k)),
        out_shape=jax.ShapeDtypeStruct((b, l, d), F32),
        scratch_shapes=[pltpu.VMEM((bb * tl, d), BF16)],
        compiler_params=_params("arbitrary", "arbitrary", "arbitrary"),
        name="proj",
    )(a, w, gate, res)


def _mlp_kernel(x_ref, g_ref, sc_ref, sh_ref, gate_ref, wu_ref, wd_ref, o_ref, h_s):
    f = pl.program_id(2)
    bb, tl, d = x_ref.shape
    nchunk = 512

    @pl.when(f == 0)
    def _():
        _store_norm_mod(x_ref, g_ref, sc_ref, sh_ref, h_s)
        o_ref[...] = jnp.zeros_like(o_ref)

    a = jnp.maximum(_dot(h_s[...], wu_ref[...].astype(BF16)), 0.0)
    a = (a * a).astype(BF16)
    for c in range(d // nchunk):
        sl = slice(c * nchunk, (c + 1) * nchunk)
        o_ref[:, :, sl] += _dot(a, wd_ref[:, sl].astype(BF16)).reshape(bb, tl, nchunk)

    @pl.when(f == pl.num_programs(2) - 1)
    def _():
        o_ref[...] = x_ref[...] + gate_ref[...] * o_ref[...]


def _mlp(x, g, scale, shift, gate, w_up, w_down, *, tl):
    b, l, d = x.shape
    ff = w_up.shape[1]
    tf = 512
    bb = _batch_block(b, l, tl)
    vec = pl.BlockSpec((bb, 1, d), lambda i, j, k: (i, 0, 0))
    return pl.pallas_call(
        _mlp_kernel,
        grid=(b // bb, l // tl, ff // tf),
        in_specs=[pl.BlockSpec((bb, tl, d), lambda i, j, k: (i, j, 0)),
                  pl.BlockSpec((1, d), lambda i, j, k: (0, 0)),
                  vec, vec, vec,
                  pl.BlockSpec((d, tf), lambda i, j, k: (0, k)),
                  pl.BlockSpec((tf, d), lambda i, j, k: (k, 0))],
        out_specs=pl.BlockSpec((bb, tl, d), lambda i, j, k: (i, j, 0)),
        out_shape=jax.ShapeDtypeStruct((b, l, d), F32),
        scratch_shapes=[pltpu.VMEM((bb * tl, d), BF16)],
        compiler_params=_params("arbitrary", "arbitrary", "arbitrary"),
        name="mlp",
    )(x, g.reshape(1, d), scale, shift, gate, w_up, w_down)


def _final_norm_kernel(x_ref, g_ref, o_ref):
    x = x_ref[...]
    o_ref[...] = x * lax.rsqrt(jnp.mean(x * x, axis=-1, keepdims=True) + NORM_EPS) * g_ref[...]


def _final_norm(x, g):
    b, l, d = x.shape
    tl = min(l, 512)
    return pl.pallas_call(
        _final_norm_kernel,
        grid=(b, l // tl),
        in_specs=[pl.BlockSpec((1, tl, d), lambda i, j: (i, j, 0)),
                  pl.BlockSpec((1, d), lambda i, j: (0, 0))],
        out_specs=pl.BlockSpec((1, tl, d), lambda i, j: (i, j, 0)),
        out_shape=jax.ShapeDtypeStruct((b, l, d), F32),
        compiler_params=_params("arbitrary", "arbitrary"),
        name="final_norm",
    )(x, g.reshape(1, d))


def _online_update(s, v, m_s, l_s, acc_s):
    m_prev = m_s[...]
    m_new = jnp.maximum(m_prev, jnp.max(s, axis=1, keepdims=True))
    alpha = jnp.exp(m_prev - m_new)
    p = jnp.exp(s - m_new)
    l_s[...] = alpha * l_s[...] + jnp.sum(p, axis=1, keepdims=True)
    acc_s[...] = alpha * acc_s[...] + _dot(p.astype(BF16), v)
    m_s[...] = m_new


def _lane_column(x, j):
    lane = lax.broadcasted_iota(jnp.int32, x.shape, 1)
    return jnp.sum(jnp.where(lane == j, x, 0.0), axis=1, keepdims=True)


def _topk_bias(gate, n_valid, k):
    lane = lax.broadcasted_iota(jnp.int32, gate.shape, 1).astype(F32)
    g = jnp.where(lane < n_valid, gate, NEG)
    bias = jnp.full(gate.shape, NEG, F32)
    for _ in range(k):
        mx = jnp.max(g, axis=1, keepdims=True)
        first = jnp.min(jnp.where(g == mx, lane, float(gate.shape[1])), axis=1, keepdims=True)
        pick = (lane == first) & (mx > 0.5 * NEG)
        bias = jnp.where(pick, 0.0, bias)
        g = jnp.where(lane == first, NEG, g)
    return bias


def _diff_lambda(lam, lam_init):
    return (jnp.exp(jnp.sum(lam[0:1] * lam[1:2], axis=1, keepdims=True))
            - jnp.exp(jnp.sum(lam[2:3] * lam[3:4], axis=1, keepdims=True)) + lam_init)


def _sub_norm(o, gsub, lam_init):
    return o * lax.rsqrt(jnp.mean(o * o, axis=-1, keepdims=True) + SUBLN_EPS) * gsub * (1.0 - lam_init)


def _win_prompt_kernel(*refs, nw):
    in_refs, (o_ref, kcat_s, vcat_s, og_s, lse_s) = refs[:5 * N_GROUPS], refs[5 * N_GROUPS:]
    t = pl.program_id(2)
    tile = o_ref.shape[0]
    scale = HEAD_DIM ** -0.5
    r = lax.broadcasted_iota(jnp.int32, (nw, 2 * nw), 0)
    c = lax.broadcasted_iota(jnp.int32, (nw, 2 * nw), 1)
    dist = nw + r - c
    band = (dist >= 0) & (dist <= nw)

    for g, (window, dil) in enumerate(DIL_GROUPS):
        q_ref, kc_ref, kp_ref, vc_ref, vp_ref = in_refs[5 * g:5 * g + 5]
        kcat_s[0:window, :] = kp_ref[...]
        kcat_s[window:window + tile, :] = kc_ref[...]
        vcat_s[0:window, :] = vp_ref[...]
        vcat_s[window:window + tile, :] = vc_ref[...]

        def block(u, carry, g=g, dil=dil, window=window, q_ref=q_ref):
            start = u % dil + (u // dil) * window
            if dil == 1:
                rows_q = pl.ds(start, nw)
                rows_k = pl.ds(start, 2 * nw)
            else:
                rows_q = pl.ds(start, nw, stride=dil)
                rows_k = pl.ds(start, 2 * nw, stride=dil)
            q = (q_ref[rows_q, :] * scale).astype(BF16)
            k = kcat_s[rows_k, :].astype(BF16)
            v = vcat_s[rows_k, :].astype(BF16)
            first_key = jnp.where((t > 0) | (u >= dil), 0, nw)
            s = _dot_nt(q, k) + jnp.where(band & (c >= first_key), 0.0, NEG)
            m = jnp.max(s, axis=1, keepdims=True)
            p = jnp.exp(s - m)
            l = jnp.sum(p, axis=1, keepdims=True)
            og_s[g, rows_q, :] = _dot(p.astype(BF16), v) * (1.0 / l)
            lse_s[g, rows_q, :] = jnp.broadcast_to(m + jnp.log(l), (nw, LANES))
            return carry

        lax.fori_loop(0, tile // nw, block, 0)

    def merge(rows):
        lses = [lse_s[g, rows, :] for g in range(N_GROUPS)]
        m = jnp.maximum(jnp.maximum(lses[0], lses[1]), lses[2])
        es = [jnp.exp(v - m) for v in lses]
        inv = 1.0 / (es[0] + es[1] + es[2])
        o_ref[rows, :] = sum(es[g] * og_s[g, rows, :] for g in range(N_GROUPS)) * inv

    _for_row_chunks(1, tile, merge)


def _win_prompt(qkv):
    b, _, nh, l, hd = qkv.shape
    tile = WIN_TILE
    nw = DIL_GROUPS[0][0] // DIL_GROUPS[0][1]
    specs = []
    for g, (window, dil) in enumerate(DIL_GROUPS):
        assert window // dil == nw and tile % window == 0

        def cur(t, g=g):
            return pl.BlockSpec((None, None, None, tile, hd), lambda bi, h, i: (bi, 3 * g + t, h, i, 0))

        def prev(t, g=g, per=tile // window, window=window):
            return pl.BlockSpec((None, None, None, window, hd),
                                lambda bi, h, i: (bi, 3 * g + t, h, jnp.maximum(i * per - 1, 0), 0))

        specs += [cur(0), cur(1), prev(1), cur(2), prev(2)]
    max_w = DIL_GROUPS[-1][0]
    return pl.pallas_call(
        functools.partial(_win_prompt_kernel, nw=nw),
        grid=(b, nh, l // tile),
        in_specs=specs,
        out_specs=pl.BlockSpec((None, None, tile, hd), lambda bi, h, i: (bi, h, i, 0)),
        out_shape=jax.ShapeDtypeStruct((b, nh, l, hd), F32),
        scratch_shapes=[pltpu.VMEM((max_w + tile, hd), F32), pltpu.VMEM((max_w + tile, hd), F32),
                        pltpu.VMEM((N_GROUPS, tile, hd), F32), pltpu.VMEM((N_GROUPS, tile, LANES), F32)],
        compiler_params=_params("arbitrary", "arbitrary", "arbitrary"),
        name="win_prompt",
    )(*([qkv] * (5 * N_GROUPS)))


def _flash_prompt_kernel(*refs, diff, lam_init, tq):
    if diff:
        lam_ref, gsub_ref, q_ref, k_ref, v_ref, o_ref, qs_s, m_s, l_s, acc_s = refs
    else:
        km_ref, q_ref, k_ref, v_ref, o_ref, qs_s, m_s, l_s, acc_s = refs
    qi = pl.program_id(2)
    rows = qs_s.shape[0]
    lane = lax.broadcasted_iota(jnp.int32, (tq, HEAD_DIM), 1)
    q = q_ref[...]
    if diff:
        qsc = q * (DIFF_HALF ** -0.5)
        qs_s[...] = jnp.concatenate([jnp.where(lane < DIFF_HALF, qsc, 0.0),
                                     jnp.where(lane >= DIFF_HALF, qsc, 0.0)], axis=0).astype(BF16)
    else:
        qs_s[...] = (q * (HEAD_DIM ** -0.5)).astype(BF16)
        sel_bias = _topk_bias(_dot_nt_3pass(q, km_ref[...]), qi, MOBA_TOPK)
    m_s[...] = jnp.full(m_s.shape, NEG, F32)
    l_s[...] = jnp.zeros(l_s.shape, F32)
    acc_s[...] = jnp.zeros(acc_s.shape, F32)

    def past_block(j, carry):
        rows_k = pl.ds(pl.multiple_of(j * tq, tq), tq)
        s = _dot_nt(qs_s[...], k_ref[rows_k, :].astype(BF16))
        if not diff:
            s = s + _lane_column(sel_bias, j)
        _online_update(s, v_ref[rows_k, :].astype(BF16), m_s, l_s, acc_s)
        return carry

    lax.fori_loop(0, qi, past_block, 0)

    rows_k = pl.ds(pl.multiple_of(qi * tq, tq), tq)
    r = lax.broadcasted_iota(jnp.int32, (rows, tq), 0) % tq
    c = lax.broadcasted_iota(jnp.int32, (rows, tq), 1)
    s = _dot_nt(qs_s[...], k_ref[rows_k, :].astype(BF16)) + jnp.where(c <= r, 0.0, NEG)
    _online_update(s, v_ref[rows_k, :].astype(BF16), m_s, l_s, acc_s)

    o = acc_s[...] * (1.0 / l_s[...])
    if diff:
        o = o[:tq] - _diff_lambda(lam_ref[...], lam_init) * o[tq:]
        o = _sub_norm(o, gsub_ref[...], lam_init)
    o_ref[...] = o


def _flash_prompt(qkv, *, diff, lam=None, gsub=None, lam_init=0.0, kmean=None):
    b, _, nh, l, hd = qkv.shape
    tq = MOBA_BLOCK
    rows = 2 * tq if diff else tq
    qkv_specs = [pl.BlockSpec((None, None, None, tq, hd), lambda bi, h, i: (bi, 0, h, i, 0)),
                 pl.BlockSpec((None, None, None, l, hd), lambda bi, h, i: (bi, 1, h, 0, 0)),
                 pl.BlockSpec((None, None, None, l, hd), lambda bi, h, i: (bi, 2, h, 0, 0))]
    if diff:
        extra = [lam, gsub.reshape(1, hd)]
        extra_specs = [pl.BlockSpec(lam.shape, lambda bi, h, i: (0, 0)),
                       pl.BlockSpec((1, hd), lambda bi, h, i: (0, 0))]
    else:
        extra = [kmean]
        extra_specs = [pl.BlockSpec((None, None, kmean.shape[2], hd), lambda bi, h, i: (bi, h, 0, 0))]
    return pl.pallas_call(
        functools.partial(_flash_prompt_kernel, diff=diff, lam_init=lam_init, tq=tq),
        grid=(b, nh, l // tq),
        in_specs=extra_specs + qkv_specs,
        out_specs=pl.BlockSpec((None, None, tq, hd), lambda bi, h, i: (bi, h, i, 0)),
        out_shape=jax.ShapeDtypeStruct((b, nh, l, hd), F32),
        scratch_shapes=[pltpu.VMEM((rows, hd), BF16), pltpu.VMEM((rows, 1), F32),
                        pltpu.VMEM((rows, 1), F32), pltpu.VMEM((rows, hd), F32)],
        compiler_params=_params("arbitrary", "arbitrary", "arbitrary"),
        name="flash_diff" if diff else "flash_moba",
    )(*extra, qkv, qkv, qkv)


def _block_mean_kernel(k_ref, o_ref):
    l, hd = k_ref.shape
    o_ref[...] = jnp.sum(k_ref[...].reshape(l // MOBA_BLOCK, MOBA_BLOCK, hd), axis=1) * (1.0 / MOBA_BLOCK)


def _block_mean_prompt(qkv):
    b, _, nh, l, hd = qkv.shape
    nb = l // MOBA_BLOCK
    return pl.pallas_call(
        _block_mean_kernel,
        grid=(b, nh),
        in_specs=[pl.BlockSpec((None, None, None, l, hd), lambda bi, h: (bi, 1, h, 0, 0))],
        out_specs=pl.BlockSpec((None, None, nb, hd), lambda bi, h: (bi, h, 0, 0)),
        out_shape=jax.ShapeDtypeStruct((b, nh, nb, hd), F32),
        compiler_params=_params("arbitrary", "arbitrary"),
        name="block_mean",
    )(qkv)


def _decode_kernel(*refs, mode, n_pg, paged, lam_init):
    refs = list(refs)
    if paged:
        refs.pop(0)
    q_ref = refs.pop(0)
    pg_refs = [refs.pop(0) for _ in range(n_pg)]
    kn_ref, vn_ref = refs.pop(0), refs.pop(0)
    bias_ref, bnew_ref = refs.pop(0), refs.pop(0)
    km_ref = refs.pop(0) if mode == "moba" else None
    if mode == "diff":
        lam_ref, gsub_ref = refs.pop(0), refs.pop(0)
    o_ref = refs.pop(0)
    lse_ref = refs.pop(0) if mode == "win" else None
    q_s, kf_s, vf_s, m_s, l_s, acc_s = refs[:6]
    sel_s = refs[6] if mode == "moba" else None

    s_id = pl.program_id(1)
    nh, n, hd = q_ref.shape
    rows = q_s.shape[0]
    half = rows // 2
    hpb = nh // 2
    tk = pg_refs[0].shape[0]
    scale = (DIFF_HALF if mode == "diff" else HEAD_DIM) ** -0.5

    @pl.when(s_id == 0)
    def _():
        lane = lax.broadcasted_iota(jnp.int32, (n, hd), 1)
        pieces = []
        for h in range(nh):
            qh = q_ref[h] * scale
            if mode == "diff":
                pieces += [jnp.where(lane < DIFF_HALF, qh, 0.0), jnp.where(lane >= DIFF_HALF, qh, 0.0)]
            else:
                pieces.append(qh)
        q_s[...] = jnp.concatenate(pieces, axis=0).astype(BF16)
        m_s[...] = jnp.full(m_s.shape, NEG, F32)
        l_s[...] = jnp.zeros(l_s.shape, F32)
        acc_s[...] = jnp.zeros(acc_s.shape, F32)
        if mode == "moba":
            n_blk = km_ref.shape[0] // nh
            for h in range(nh):
                gate = _dot_nt_3pass(q_ref[h], km_ref[pl.ds(h, n_blk, stride=nh), :])
                sel_s[h * n:(h + 1) * n, :] = _topk_bias(gate, n_blk, MOBA_TOPK)

    for hb in range(2):
        hs = slice(hb * hpb, (hb + 1) * hpb)
        rs = slice(hb * half, (hb + 1) * half)
        for r in range(n_pg):
            dst = slice(r * tk * hpb, (r + 1) * tk * hpb)
            kf_s[dst, :] = pg_refs[r][:, 0, hs, :].reshape(tk * hpb, hd).astype(BF16)
            vf_s[dst, :] = pg_refs[r][:, 1, hs, :].reshape(tk * hpb, hd).astype(BF16)
        s = _dot_nt(q_s[rs, :], kf_s[...]) + bias_ref[...]
        if mode == "moba":
            pages_per_blk = MOBA_BLOCK // tk
            cols = []
            for r in range(n_pg // pages_per_blk):
                col = _lane_column(sel_s[rs, :], s_id * (n_pg // pages_per_blk) + r)
                cols.append(jnp.broadcast_to(col, (half, MOBA_BLOCK * hpb)))
            s = s + jnp.concatenate(cols, axis=1)
        _online_update(s, vf_s[...], m_s.at[rs, :], l_s.at[rs, :], acc_s.at[rs, :])

    @pl.when(s_id == pl.num_programs(1) - 1)
    def _():
        pad = jnp.zeros((LANES - hpb * n, hd), F32)
        for hb in range(2):
            hs = slice(hb * hpb, (hb + 1) * hpb)
            rs = slice(hb * half, (hb + 1) * half)
            kn = jnp.concatenate([kn_ref[hs].reshape(hpb * n, hd), pad], axis=0).astype(BF16)
            vn = jnp.concatenate([vn_ref[hs].reshape(hpb * n, hd), pad], axis=0).astype(BF16)
            _online_update(_dot_nt(q_s[rs, :], kn) + bnew_ref[...], vn,
                           m_s.at[rs, :], l_s.at[rs, :], acc_s.at[rs, :])
        lsum = l_s[...]
        o = acc_s[...] * (1.0 / lsum)
        if mode == "diff":
            lam_full = _diff_lambda(lam_ref[...], lam_init)
            for h in range(nh):
                oh = o[2 * h * n:(2 * h + 1) * n] - lam_full * o[(2 * h + 1) * n:(2 * h + 2) * n]
                o_ref[h] = _sub_norm(oh, gsub_ref[...], lam_init)
        else:
            o_ref[...] = o.reshape(nh, n, hd)
        if mode == "win":
            lse_ref[...] = jnp.broadcast_to(m_s[...] + jnp.log(lsum), (rows, LANES)).reshape(nh, n, LANES)


def _head_match_bias(half_rows, rows_per_head, n_keys, hpb):
    r = np.arange(half_rows)[:, None] // rows_per_head
    c = np.arange(n_keys * hpb)[None, :] % hpb
    return np.where(r == c, 0.0, NEG)


def _decode_attn(qkv, gt, cache, layer, *, mode, page_table=None, dil=1, window=None,
                 kmean=None, lam=None, gsub=None, lam_init=0.0):
    b, _, nh, n, hd = qkv.shape
    hpb = nh // 2
    comps = 2 if mode == "diff" else 1
    rows = nh * comps * n
    half = rows // 2
    paged = page_table is not None
    tk = PAGE_SIZE
    total_pages = page_table.shape[1] if paged else cache.shape[2] // tk
    n_pg = min(PAGES_PER_STEP, total_pages)
    steps = total_pages // n_pg
    blk = (None, None, tk, 2, nh, hd)
    if paged:
        pg_specs = [pl.BlockSpec(blk, functools.partial(
            lambda bi, s, pt, r: (layer, pt[bi, s * n_pg + r], 0, 0, 0, 0), r=r)) for r in range(n_pg)]
    else:
        pg_specs = [pl.BlockSpec(blk, functools.partial(
            lambda bi, s, r: (layer, bi, s * n_pg + r, 0, 0, 0), r=r)) for r in range(n_pg)]

    def fixed(shape, idx):
        if paged:
            return pl.BlockSpec(shape, lambda bi, s, pt: idx(bi, s))
        return pl.BlockSpec(shape, lambda bi, s: idx(bi, s))

    qi = np.arange(half)[:, None] % n
    bias = _head_match_bias(half, comps * n, total_pages * tk if mode == "win" else n_pg * tk, hpb)
    cn = np.arange(LANES)[None, :]
    new_ok = (cn < hpb * n) & (cn // n == np.arange(half)[:, None] // (comps * n)) & (cn % n <= qi)
    if mode == "win":
        n_buf = total_pages * tk
        dist = n_buf + qi - np.arange(n_buf * hpb)[None, :] // hpb
        bias = np.where((dist % dil == 0) & (dist <= window), bias, NEG)
        new_ok &= (qi - cn % n) % dil == 0
    bias = jnp.asarray(bias, F32)
    bias_new = jnp.asarray(np.where(new_ok, 0.0, NEG), F32)

    head_blk = (None, None, nh, n, hd)
    args = [qkv] + [cache] * n_pg + [qkv, qkv, bias, bias_new]
    specs = [fixed(head_blk, lambda bi, s: (bi, gt, 0, 0, 0))] + pg_specs + [
        fixed(head_blk, lambda bi, s: (bi, gt + 1, 0, 0, 0)),
        fixed(head_blk, lambda bi, s: (bi, gt + 2, 0, 0, 0)),
        fixed((half, n_pg * tk * hpb), (lambda bi, s: (0, s)) if mode == "win" else (lambda bi, s: (0, 0))),
        fixed((half, LANES), lambda bi, s: (0, 0))]
    if mode == "moba":
        n_blk = kmean.shape[1]
        args.append(kmean.reshape(b, n_blk * nh, hd))
        specs.append(fixed((None, n_blk * nh, hd), lambda bi, s: (bi, 0, 0)))
    if mode == "diff":
        args += [lam, gsub.reshape(1, hd)]
        specs += [fixed(lam.shape, lambda bi, s: (0, 0)), fixed((1, hd), lambda bi, s: (0, 0))]
    scratch = [pltpu.VMEM((rows, hd), BF16),
               pltpu.VMEM((n_pg * tk * hpb, hd), BF16), pltpu.VMEM((n_pg * tk * hpb, hd), BF16),
               pltpu.VMEM((rows, 1), F32), pltpu.VMEM((rows, 1), F32), pltpu.VMEM((rows, hd), F32)]
    if mode == "moba":
        scratch.append(pltpu.VMEM((rows, n_blk), F32))
    out_blk = (None, nh, n, hd)
    out_shape = [jax.ShapeDtypeStruct((b, nh, n, hd), F32)]
    out_specs = [fixed(out_blk, lambda bi, s: (bi, 0, 0, 0))]
    if mode == "win":
        out_shape.append(jax.ShapeDtypeStruct((b, nh, n, LANES), F32))
        out_specs.append(fixed(out_blk, lambda bi, s: (bi, 0, 0, 0)))
    kern = functools.partial(_decode_kernel, mode=mode, n_pg=n_pg, paged=paged, lam_init=lam_init)
    name = "decode_" + mode
    if paged:
        grid_spec = pltpu.PrefetchScalarGridSpec(num_scalar_prefetch=1, grid=(b, steps), in_specs=specs,
                                                 out_specs=out_specs, scratch_shapes=scratch)
        out = pl.pallas_call(kern, grid_spec=grid_spec, out_shape=out_shape, name=name,
                             compiler_params=_params("arbitrary", "arbitrary"))(page_table, *args)
    else:
        out = pl.pallas_call(kern, grid=(b, steps), in_specs=specs, out_specs=out_specs, out_shape=out_shape,
                             scratch_shapes=scratch, name=name,
                             compiler_params=_params("arbitrary", "arbitrary"))(*args)
    return out if mode == "win" else out[0]


def _merge_groups_kernel(*refs):
    o_refs, lse_refs, out_ref = refs[:N_GROUPS], refs[N_GROUPS:2 * N_GROUPS], refs[2 * N_GROUPS]
    lses = [r[...] for r in lse_refs]
    m = jnp.maximum(jnp.maximum(lses[0], lses[1]), lses[2])
    es = [jnp.exp(v - m) for v in lses]
    inv = 1.0 / (es[0] + es[1] + es[2])
    out_ref[...] = sum(es[g] * o_refs[g][...] for g in range(N_GROUPS)) * inv


def _merge_groups(outs, lses):
    shape = outs[0].shape
    spec = pl.BlockSpec(shape, lambda i: (0,) * len(shape))
    return pl.pallas_call(
        _merge_groups_kernel, grid=(1,), in_specs=[spec] * (2 * N_GROUPS), out_specs=spec,
        out_shape=jax.ShapeDtypeStruct(shape, F32), compiler_params=_params("arbitrary"),
        name="merge_groups",
    )(*outs, *lses)


def _paged_mean_kernel(pt_ref, *refs, n_pages, pages_per_blk):
    o_ref = refs[n_pages]
    for j in range(n_pages // pages_per_blk):
        tot = sum(jnp.sum(refs[j * pages_per_blk + r][...], axis=0) for r in range(pages_per_blk))
        o_ref[j] = tot * (1.0 / MOBA_BLOCK)


def _paged_block_mean(cache, layer, page_table):
    b, total_pages = page_table.shape
    tk, _, nh, hd = cache.shape[2:]
    pages_per_blk = MOBA_BLOCK // tk
    n_pages = 8
    steps = total_pages // n_pages
    specs = [pl.BlockSpec((None, None, tk, None, nh, hd), functools.partial(
        lambda bi, s, pt, r: (layer, pt[bi, s * n_pages + r], 0, 0, 0, 0), r=r)) for r in range(n_pages)]
    return pl.pallas_call(
        functools.partial(_paged_mean_kernel, n_pages=n_pages, pages_per_blk=pages_per_blk),
        grid_spec=pltpu.PrefetchScalarGridSpec(
            num_scalar_prefetch=1, grid=(b, steps), in_specs=specs,
            out_specs=pl.BlockSpec((None, n_pages // pages_per_blk, nh, hd), lambda bi, s, pt: (bi, s, 0, 0))),
        out_shape=jax.ShapeDtypeStruct((b, total_pages // pages_per_blk, nh, hd), F32),
        compiler_params=_params("arbitrary", "arbitrary"),
        name="paged_block_mean",
    )(page_table, *([cache] * n_pages))


def _rope_tables(pos, half, reps=1):
    inv = ROPE_THETA ** (-np.arange(0, 2 * half, 2, dtype=np.float64) / (2 * half))
    ang = np.asarray(pos, np.float64)[:, None] * inv[None, :]
    cos = np.concatenate([np.cos(ang), np.cos(ang)], axis=1)
    sin = np.concatenate([-np.sin(ang), np.sin(ang)], axis=1)
    tile = LANES // (2 * half)
    cos = np.tile(np.tile(cos, (1, tile)), (reps, 1))
    sin = np.tile(np.tile(sin, (1, tile)), (reps, 1))
    return jnp.asarray(cos, F32), jnp.asarray(sin, F32)


def _rows_of(kv):
    return jnp.transpose(kv, (0, 3, 1, 2, 4))


def kernel(x_prompt, x_sample, c_prompt, c_sample, cache_win1_kv, cache_win2_kv, cache_win3_kv, cache_kv,
           page_table, norm1_g, norm2_g, w_mod, b_mod, w_qkv_a, w_o_a, w_qkv_b, w_o_b, lambda_b, subln_g_b,
           w_qkv_c, w_o_c, w_up, w_down, final_g):
    bp, seq, d = x_prompt.shape
    bs, n_new, _ = x_sample.shape
    depth = w_mod.shape[0]
    past_len = page_table.shape[1] * PAGE_SIZE
    assert past_len % MOBA_BLOCK == 0 and seq % WIN_TILE == 0 and n_new == SUBLANES
    assert cache_kv.shape[2] == PAGE_SIZE
    pos_p = np.arange(seq)
    pos_s = past_len + np.arange(n_new)
    tl_p = 1024

    rope_p = {h: _rope_tables(pos_p, h) for h in (HEAD_DIM // 2, DIFF_HALF // 2)}
    rope_s = {h: _rope_tables(pos_s, h, reps=bs) for h in (HEAD_DIM // 2, DIFF_HALF // 2)}

    rows_c = 2 * SUBLANES
    c_all = jnp.concatenate([c_prompt, c_sample, jnp.zeros((rows_c - bp - bs, d), F32)], axis=0)
    mods = _modulation(c_all, w_mod, b_mod)
    win_caches = (cache_win1_kv, cache_win2_kv, cache_win3_kv)

    xp, xs = x_prompt, x_sample
    win_p = [[] for _ in range(N_GROUPS)]
    win_s = [[] for _ in range(N_GROUPS)]
    kv_p, kv_s = [], []
    for i in range(depth):
        kind = i % N_MIXERS
        j = i // N_MIXERS
        mp = [mods[i, :bp, t * d:(t + 1) * d][:, None, :] for t in range(6)]
        ms = [mods[i, bp:bp + bs, t * d:(t + 1) * d][:, None, :] for t in range(6)]
        if kind == 0:
            half = HEAD_DIM // 2
            qkv_p = _qkv(xp, norm1_g[i], mp[1], mp[0], *rope_p[half], w_qkv_a[j], half=half, tl=tl_p)
            qkv_s = _qkv(xs, norm1_g[i], ms[1], ms[0], *rope_s[half], w_qkv_a[j], half=half, tl=n_new)
            o_p = _win_prompt(qkv_p)
            outs_s, lses_s = [], []
            for g, (window, dil) in enumerate(DIL_GROUPS):
                keep = min(window, seq)
                win_p[g].append(_rows_of(qkv_p[:, 3 * g + 1:3 * g + 3, :, seq - keep:]))
                o_g, lse_g = _decode_attn(qkv_s, 3 * g, win_caches[g], j, mode="win", dil=dil, window=window)
                outs_s.append(o_g)
                lses_s.append(lse_g)
                new_rows = _rows_of(qkv_s[:, 3 * g + 1:3 * g + 3])
                win_s[g].append(jnp.concatenate([win_caches[g][j][:, n_new:], new_rows], axis=1))
            o_s = _merge_groups(outs_s, lses_s)
            w_o = w_o_a[j]
        else:
            slot = i - (i + 2) // N_MIXERS
            if kind == 1:
                half = DIFF_HALF // 2
                w_qkv, w_o = w_qkv_b[j], w_o_b[j]
            else:
                half = HEAD_DIM // 2
                w_qkv, w_o = w_qkv_c[j], w_o_c[j]
            qkv_p = _qkv(xp, norm1_g[i], mp[1], mp[0], *rope_p[half], w_qkv, half=half, tl=tl_p)
            qkv_s = _qkv(xs, norm1_g[i], ms[1], ms[0], *rope_s[half], w_qkv, half=half, tl=n_new)
            if kind == 1:
                lam_init = 0.8 - 0.6 * math.exp(-0.3 * i)
                o_p = _flash_prompt(qkv_p, diff=True, lam=lambda_b[j], gsub=subln_g_b[j], lam_init=lam_init)
                o_s = _decode_attn(qkv_s, 0, cache_kv, slot, mode="diff", page_table=page_table,
                                   lam=lambda_b[j], gsub=subln_g_b[j], lam_init=lam_init)
            else:
                o_p = _flash_prompt(qkv_p, diff=False, kmean=_block_mean_prompt(qkv_p))
                o_s = _decode_attn(qkv_s, 0, cache_kv, slot, mode="moba", page_table=page_table,
                                   kmean=_paged_block_mean(cache_kv, slot, page_table))
            kv_p.append(_rows_of(qkv_p[:, 1:3]))
            kv_s.append(_rows_of(qkv_s[:, 1:3]))
        xp = _proj(o_p, w_o, mp[2], xp, tl=tl_p)
        xs = _proj(o_s, w_o, ms[2], xs, tl=n_new)
        xp = _mlp(xp, norm2_g[i], mp[4], mp[3], mp[5], w_up[i], w_down[i], tl=512)
        xs = _mlp(xs, norm2_g[i], ms[4], ms[3], ms[5], w_up[i], w_down[i], tl=n_new)
    y_prompt = _final_norm(xp, final_g)
    y_sample = _final_norm(xs, final_g)
    return (y_prompt, y_sample,
            jnp.stack(win_p[0]), jnp.stack(win_p[1]), jnp.stack(win_p[2]), jnp.stack(kv_p),
            jnp.stack(win_s[0]), jnp.stack(win_s[1]), jnp.stack(win_s[2]), jnp.stack(kv_s))
```

```python
import functools
import math

import numpy as np
import jax
import jax.numpy as jnp
from jax import lax
from jax.experimental import pallas as pl
from jax.experimental.pallas import tpu as pltpu

F32 = jnp.float32
BF16 = jnp.bfloat16

HEAD_DIM = 128
DIL_GROUPS = ((128, 1), (512, 4), (2048, 16))
N_GROUPS = len(DIL_GROUPS)
DIFF_HALF = HEAD_DIM // 2
MOBA_BLOCK = 256
MOBA_TOPK = 3
PAGE_SIZE = 128
ROPE_THETA = 10000.0
NORM_EPS = 1e-6
SUBLN_EPS = 1e-5
N_MIXERS = 3

LANES = 128
SUBLANES = 8
NEG = -1e30
VMEM_LIMIT = 56 * 1024 * 1024
PAGES_PER_STEP = 4
ROW_CHUNK = 128
WIN_TILE = 2048
WIN_UNROLL = 4
FLASH_CHUNK = 1024
FLASH_ROWS = 128


def _params(*sem):
    return pltpu.CompilerParams(dimension_semantics=sem, vmem_limit_bytes=VMEM_LIMIT)


def _dot(a, b):
    return jnp.dot(a, b, preferred_element_type=F32)


def _dot_nt(a, b):
    return lax.dot_general(a, b, (((1,), (1,)), ((), ())), preferred_element_type=F32)


def _split_bf16(x):
    hi = x.astype(BF16)
    lo = (x - hi.astype(F32)).astype(BF16)
    return hi, lo


def _dot_nt_3pass(a, b):
    ah, al = _split_bf16(a)
    bh, bl = _split_bf16(b)
    return _dot_nt(ah, bh) + _dot_nt(ah, bl) + _dot_nt(al, bh)


def _norm_mod(x, g, scale, shift):
    y = x * lax.rsqrt(jnp.mean(x * x, axis=-1, keepdims=True) + NORM_EPS) * g
    return y * (1.0 + scale) + shift


def _for_row_chunks(bb, tl, body):
    if bb > 1 or tl <= ROW_CHUNK:
        body(slice(None))
        return

    def step(c, carry):
        body(pl.ds(pl.multiple_of(c * ROW_CHUNK, ROW_CHUNK), ROW_CHUNK))
        return carry

    lax.fori_loop(0, tl // ROW_CHUNK, step, 0)


def _store_norm_mod(x_ref, g_ref, sc_ref, sh_ref, h_s):
    bb, tl, d = x_ref.shape

    def body(rows):
        h = _norm_mod(x_ref[:, rows, :], g_ref[...], sc_ref[...], sh_ref[...])
        h_s[rows, :] = h.reshape(-1, d).astype(BF16)

    _for_row_chunks(bb, tl, body)


def _batch_block(b, l, tl):
    return b if tl == l and b * l <= 1024 else 1


def _mod_kernel(c_ref, w_ref, b_ref, o_ref):
    c = c_ref[...]
    a = (c / (1.0 + jnp.exp(-c))).astype(BF16)
    o_ref[...] = _dot(a, w_ref[...].astype(BF16)) + b_ref[...]


def _modulation(c_all, w_mod, b_mod):
    depth, d, n = w_mod.shape
    rows = c_all.shape[0]
    tn = 1024
    return pl.pallas_call(
        _mod_kernel,
        grid=(depth, n // tn),
        in_specs=[pl.BlockSpec((rows, d), lambda i, j: (0, 0)),
                  pl.BlockSpec((None, d, tn), lambda i, j: (i, 0, j)),
                  pl.BlockSpec((None, 1, tn), lambda i, j: (i, 0, j))],
        out_specs=pl.BlockSpec((None, rows, tn), lambda i, j: (i, 0, j)),
        out_shape=jax.ShapeDtypeStruct((depth, rows, n), F32),
        compiler_params=_params("arbitrary", "arbitrary"),
        name="modulation",
    )(c_all, w_mod, b_mod.reshape(depth, 1, n))


def _rope_chunk(x, cos, sin, half):
    if 2 * half == LANES:
        rot = pltpu.roll(x, half, 1)
    else:
        lane = lax.broadcasted_iota(jnp.int32, x.shape, 1)
        rot = jnp.where(lane % (2 * half) < half, pltpu.roll(x, LANES - half, 1), pltpu.roll(x, half, 1))
    return x * cos + rot * sin


def _qkv_kernel(x_ref, g_ref, sc_ref, sh_ref, cos_ref, sin_ref, w_ref, o_ref, h_s, *, half, tiles_per_kind):
    n = pl.program_id(2)
    bb, tl, d = x_ref.shape
    tn = w_ref.shape[1]

    @pl.when(n == 0)
    def _():
        _store_norm_mod(x_ref, g_ref, sc_ref, sh_ref, h_s)

    acc = _dot(h_s[...], w_ref[...].astype(BF16))
    kind = (n // tiles_per_kind) % 3

    @pl.when(kind < 2)
    def _():
        cos = cos_ref[...]
        sin = sin_ref[...]
        for c in range(tn // LANES):
            chunk = _rope_chunk(acc[:, c * LANES:(c + 1) * LANES], cos, sin, half)
            o_ref[:, c, :, :] = chunk.reshape(bb, tl, LANES)

    @pl.when(kind == 2)
    def _():
        for c in range(tn // LANES):
            o_ref[:, c, :, :] = acc[:, c * LANES:(c + 1) * LANES].reshape(bb, tl, LANES)


def _qkv(x, g, scale, shift, cos, sin, w, *, half, tl):
    b, l, d = x.shape
    n = w.shape[1]
    nh = d // HEAD_DIM
    tn = 512
    hpt = tn // HEAD_DIM
    tpk = d // tn
    bb = _batch_block(b, l, tl)
    return pl.pallas_call(
        functools.partial(_qkv_kernel, half=half, tiles_per_kind=tpk),
        grid=(b // bb, l // tl, n // tn),
        in_specs=[pl.BlockSpec((bb, tl, d), lambda i, j, k: (i, j, 0)),
                  pl.BlockSpec((1, d), lambda i, j, k: (0, 0)),
                  pl.BlockSpec((bb, 1, d), lambda i, j, k: (i, 0, 0)),
                  pl.BlockSpec((bb, 1, d), lambda i, j, k: (i, 0, 0)),
                  pl.BlockSpec((bb * tl, LANES), lambda i, j, k: (j, 0)),
                  pl.BlockSpec((bb * tl, LANES), lambda i, j, k: (j, 0)),
                  pl.BlockSpec((d, tn), lambda i, j, k: (0, k))],
        out_specs=pl.BlockSpec((bb, None, hpt, tl, HEAD_DIM), lambda i, j, k: (i, k // tpk, k % tpk, j, 0)),
        out_shape=jax.ShapeDtypeStruct((b, n // d, nh, l, HEAD_DIM), F32),
        scratch_shapes=[pltpu.VMEM((bb * tl, d), BF16)],
        compiler_params=_params("arbitrary", "arbitrary", "arbitrary"),
        name="qkv",
    )(x, g.reshape(1, d), scale, shift, cos, sin, w)


def _proj_kernel(a_ref, w_ref, gate_ref, res_ref, o_ref, a_s):
    n = pl.program_id(2)
    bb, nh, tl, hd = a_ref.shape
    tn = w_ref.shape[1]

    @pl.when(n == 0)
    def _():
        def body(rows):
            for h in range(nh):
                a_s[rows, h * hd:(h + 1) * hd] = a_ref[:, h, rows, :].reshape(-1, hd).astype(BF16)

        _for_row_chunks(bb, tl, body)

    acc = _dot(a_s[...], w_ref[...].astype(BF16))
    o_ref[...] = res_ref[...] + gate_ref[...] * acc.reshape(bb, tl, tn)


def _proj(a, w, gate, res, *, tl):
    b, l, d = res.shape
    nh = a.shape[1]
    tn = 512
    bb = _batch_block(b, l, tl)
    return pl.pallas_call(
        _proj_kernel,
        grid=(b // bb, l // tl, d // tn),
        in_specs=[pl.BlockSpec((bb, nh, tl, HEAD_DIM), lambda i, j, k: (i, 0, j, 0)),
                  pl.BlockSpec((d, tn), lambda i, j, k: (0, k)),
                  pl.BlockSpec((bb, 1, tn), lambda i, j, k: (i, 0, k)),
                  pl.BlockSpec((bb, tl, tn), lambda i, j, k: (i, j, k))],
        out_specs=pl.BlockSpec((bb, tl, tn), lambda i, j, k: (i, j, k)),
        out_shape=jax.ShapeDtypeStruct((b, l, d), F32),
        scratch_shapes=[pltpu.VMEM((bb * tl, d), BF16)],
        compiler_params=_params("arbitrary", "arbitrary", "arbitrary"),
        name="proj",
    )(a, w, gate, res)


def _mlp_kernel(x_ref, g_ref, sc_ref, sh_ref, gate_ref, wu_ref, wd_ref, o_ref, h_s):
    f = pl.program_id(2)
    bb, tl, d = x_ref.shape
    nchunk = 512

    @pl.when(f == 0)
    def _():
        _store_norm_mod(x_ref, g_ref, sc_ref, sh_ref, h_s)
        o_ref[...] = jnp.zeros_like(o_ref)

    a = jnp.maximum(_dot(h_s[...], wu_ref[...].astype(BF16)), 0.0)
    a = (a * a).astype(BF16)
    for c in range(d // nchunk):
        sl = slice(c * nchunk, (c + 1) * nchunk)
        o_ref[:, :, sl] += _dot(a, wd_ref[:, sl].astype(BF16)).reshape(bb, tl, nchunk)

    @pl.when(f == pl.num_programs(2) - 1)
    def _():
        o_ref[...] = x_ref[...] + gate_ref[...] * o_ref[...]


def _mlp(x, g, scale, shift, gate, w_up, w_down, *, tl):
    b, l, d = x.shape
    ff = w_up.shape[1]
    bb = _batch_block(b, l, tl)
    big = bb * tl >= 1024
    tf = 512
    vec = pl.BlockSpec((bb, 1, d), lambda i, j, k: (i, 0, 0))
    x_mode = dict(pipeline_mode=pl.Buffered(1)) if big else {}
    return pl.pallas_call(
        _mlp_kernel,
        grid=(b // bb, l // tl, ff // tf),
        in_specs=[pl.BlockSpec((bb, tl, d), lambda i, j, k: (i, j, 0), **x_mode),
                  pl.BlockSpec((1, d), lambda i, j, k: (0, 0)),
                  vec, vec, vec,
                  pl.BlockSpec((d, tf), lambda i, j, k: (0, k)),
                  pl.BlockSpec((tf, d), lambda i, j, k: (k, 0))],
        out_specs=pl.BlockSpec((bb, tl, d), lambda i, j, k: (i, j, 0), **x_mode),
        out_shape=jax.ShapeDtypeStruct((b, l, d), F32),
        scratch_shapes=[pltpu.VMEM((bb * tl, d), BF16)],
        compiler_params=_params("arbitrary", "arbitrary", "arbitrary"),
        name="mlp",
    )(x, g.reshape(1, d), scale, shift, gate, w_up, w_down)


def _final_norm_kernel(x_ref, g_ref, o_ref):
    x = x_ref[...]
    o_ref[...] = x * lax.rsqrt(jnp.mean(x * x, axis=-1, keepdims=True) + NORM_EPS) * g_ref[...]


def _final_norm(x, g):
    b, l, d = x.shape
    tl = min(l, 512)
    return pl.pallas_call(
        _final_norm_kernel,
        grid=(b, l // tl),
        in_specs=[pl.BlockSpec((1, tl, d), lambda i, j: (i, j, 0)),
                  pl.BlockSpec((1, d), lambda i, j: (0, 0))],
        out_specs=pl.BlockSpec((1, tl, d), lambda i, j: (i, j, 0)),
        out_shape=jax.ShapeDtypeStruct((b, l, d), F32),
        compiler_params=_params("arbitrary", "arbitrary"),
        name="final_norm",
    )(x, g.reshape(1, d))


def _online_update(s, v, m_s, l_s, acc_s):
    m_prev = m_s[...]
    m_new = jnp.maximum(m_prev, jnp.max(s, axis=1, keepdims=True))
    alpha = jnp.exp(m_prev - m_new)
    p = jnp.exp(s - m_new)
    l_s[...] = alpha * l_s[...] + jnp.sum(p, axis=1, keepdims=True)
    acc_s[...] = alpha * acc_s[...] + _dot(p.astype(BF16), v)
    m_s[...] = m_new


def _lane_column(x, j):
    lane = lax.broadcasted_iota(jnp.int32, x.shape, 1)
    return jnp.sum(jnp.where(lane == j, x, 0.0), axis=1, keepdims=True)


def _topk_bias(gate, n_valid, k):
    lane = lax.broadcasted_iota(jnp.int32, gate.shape, 1).astype(F32)
    g = jnp.where(lane < n_valid, gate, NEG)
    bias = jnp.full(gate.shape, NEG, F32)
    for _ in range(k):
        mx = jnp.max(g, axis=1, keepdims=True)
        first = jnp.min(jnp.where(g == mx, lane, float(gate.shape[1])), axis=1, keepdims=True)
        pick = (lane == first) & (mx > 0.5 * NEG)
        bias = jnp.where(pick, 0.0, bias)
        g = jnp.where(lane == first, NEG, g)
    return bias


def _diff_lambda(lam, lam_init):
    return (jnp.exp(jnp.sum(lam[0:1] * lam[1:2], axis=1, keepdims=True))
            - jnp.exp(jnp.sum(lam[2:3] * lam[3:4], axis=1, keepdims=True)) + lam_init)


def _sub_norm(o, gsub, lam_init):
    return o * lax.rsqrt(jnp.mean(o * o, axis=-1, keepdims=True) + SUBLN_EPS) * gsub * (1.0 - lam_init)


def _win_prompt_kernel(*refs, nw):
    in_refs, (o_ref, kcat_s, vcat_s, og_s, lse_s) = refs[:5 * N_GROUPS], refs[5 * N_GROUPS:]
    t = pl.program_id(2)
    tile = o_ref.shape[0]
    scale = HEAD_DIM ** -0.5
    r = lax.broadcasted_iota(jnp.int32, (nw, 2 * nw), 0)
    c = lax.broadcasted_iota(jnp.int32, (nw, 2 * nw), 1)
    dist = nw + r - c
    band = (dist >= 0) & (dist <= nw)

    for g, (window, dil) in enumerate(DIL_GROUPS):
        q_ref, kc_ref, kp_ref, vc_ref, vp_ref = in_refs[5 * g:5 * g + 5]
        kcat_s[0:window, :] = kp_ref[...]
        kcat_s[window:window + tile, :] = kc_ref[...]
        vcat_s[0:window, :] = vp_ref[...]
        vcat_s[window:window + tile, :] = vc_ref[...]

        def block(u, carry, g=g, dil=dil, window=window, q_ref=q_ref):
            start = u % dil + (u // dil) * window
            if dil == 1:
                rows_q = pl.ds(start, nw)
                rows_k = pl.ds(start, 2 * nw)
            else:
                rows_q = pl.ds(start, nw, stride=dil)
                rows_k = pl.ds(start, 2 * nw, stride=dil)
            q = (q_ref[rows_q, :] * scale).astype(BF16)
            k = kcat_s[rows_k, :].astype(BF16)
            v = vcat_s[rows_k, :].astype(BF16)
            first_key = jnp.where((t > 0) | (u >= dil), 0, nw)
            s = _dot_nt(q, k) + jnp.where(band & (c >= first_key), 0.0, NEG)
            m = jnp.max(s, axis=1, keepdims=True)
            p = jnp.exp(s - m)
            l = jnp.sum(p, axis=1, keepdims=True)
            og_s[g, rows_q, :] = _dot(p.astype(BF16), v) * (1.0 / l)
            lse_s[g, rows_q, :] = jnp.broadcast_to(m + jnp.log(l), (nw, LANES))
            return carry

        lax.fori_loop(0, tile // nw, block, 0, unroll=WIN_UNROLL)

    def merge(rows):
        lses = [lse_s[g, rows, :] for g in range(N_GROUPS)]
        m = jnp.maximum(jnp.maximum(lses[0], lses[1]), lses[2])
        es = [jnp.exp(v - m) for v in lses]
        inv = 1.0 / (es[0] + es[1] + es[2])
        o_ref[rows, :] = sum(es[g] * og_s[g, rows, :] for g in range(N_GROUPS)) * inv

    _for_row_chunks(1, tile, merge)


def _win_prompt(qkv):
    b, _, nh, l, hd = qkv.shape
    tile = WIN_TILE
    nw = DIL_GROUPS[0][0] // DIL_GROUPS[0][1]
    specs = []
    for g, (window, dil) in enumerate(DIL_GROUPS):
        assert window // dil == nw and tile % window == 0

        def cur(t, g=g):
            return pl.BlockSpec((None, None, None, tile, hd), lambda bi, h, i: (bi, 3 * g + t, h, i, 0))

        def prev(t, g=g, per=tile // window, window=window):
            return pl.BlockSpec((None, None, None, window, hd),
                                lambda bi, h, i: (bi, 3 * g + t, h, jnp.maximum(i * per - 1, 0), 0))

        specs += [cur(0), cur(1), prev(1), cur(2), prev(2)]
    max_w = DIL_GROUPS[-1][0]
    return pl.pallas_call(
        functools.partial(_win_prompt_kernel, nw=nw),
        grid=(b, nh, l // tile),
        in_specs=specs,
        out_specs=pl.BlockSpec((None, None, tile, hd), lambda bi, h, i: (bi, h, i, 0)),
        out_shape=jax.ShapeDtypeStruct((b, nh, l, hd), F32),
        scratch_shapes=[pltpu.VMEM((max_w + tile, hd), F32), pltpu.VMEM((max_w + tile, hd), F32),
                        pltpu.VMEM((N_GROUPS, tile, hd), F32), pltpu.VMEM((N_GROUPS, tile, LANES), F32)],
        compiler_params=_params("arbitrary", "arbitrary", "arbitrary"),
        name="win_prompt",
    )(*([qkv] * (5 * N_GROUPS)))


def _flash_prompt_kernel(*refs, diff, lam_init, tq):
    if diff:
        lam_ref, gsub_ref, q_ref, k_ref, v_ref, o_ref, qs_s, m_s, l_s, acc_s = refs
    else:
        km_ref, q_ref, k_ref, v_ref, o_ref, qs_s, m_s, l_s, acc_s = refs
    qi = pl.program_id(2)
    rows = qs_s.shape[0]
    lane = lax.broadcasted_iota(jnp.int32, (tq, HEAD_DIM), 1)
    q = q_ref[...]
    if diff:
        qsc = q * (DIFF_HALF ** -0.5)
        qs_s[...] = jnp.concatenate([jnp.where(lane < DIFF_HALF, qsc, 0.0),
                                     jnp.where(lane >= DIFF_HALF, qsc, 0.0)], axis=0).astype(BF16)
    else:
        qs_s[...] = (q * (HEAD_DIM ** -0.5)).astype(BF16)
        sel_bias = _topk_bias(_dot_nt_3pass(q, km_ref[...]), qi, MOBA_TOPK)
    m_s[...] = jnp.full(m_s.shape, NEG, F32)
    l_s[...] = jnp.zeros(l_s.shape, F32)
    acc_s[...] = jnp.zeros(acc_s.shape, F32)

    chunk = FLASH_CHUNK
    per_chunk = chunk // tq
    diag = qi // per_chunk

    def selection_bias(ci):
        cols = [jnp.broadcast_to(_lane_column(sel_bias, ci * per_chunk + t), (tq, tq)) for t in range(per_chunk)]
        return jnp.concatenate(cols, axis=1)

    def consume(ci, bias):
        rows_k = pl.ds(pl.multiple_of(ci * chunk, chunk), chunk)
        k = k_ref[rows_k, :].astype(BF16)
        v = v_ref[rows_k, :].astype(BF16)
        for r0 in range(0, rows, FLASH_ROWS):
            rs = slice(r0, r0 + FLASH_ROWS)
            s = _dot_nt(qs_s[rs, :], k)
            if bias is not None:
                s = s + bias[r0 % tq:r0 % tq + FLASH_ROWS]
            _online_update(s, v, m_s.at[rs, :], l_s.at[rs, :], acc_s.at[rs, :])

    def past_chunk(ci, carry):
        consume(ci, None if diff else selection_bias(ci))
        return carry

    lax.fori_loop(0, diag, past_chunk, 0)

    q_pos = qi * tq + lax.broadcasted_iota(jnp.int32, (tq, chunk), 0)
    k_pos = diag * chunk + lax.broadcasted_iota(jnp.int32, (tq, chunk), 1)
    bias = jnp.where(k_pos <= q_pos, 0.0, NEG)
    if not diff:
        bias = jnp.where(k_pos >= qi * tq, bias, selection_bias(diag))
    consume(diag, bias)

    o = acc_s[...] * (1.0 / l_s[...])
    if diff:
        o = o[:tq] - _diff_lambda(lam_ref[...], lam_init) * o[tq:]
        o = _sub_norm(o, gsub_ref[...], lam_init)
    o_ref[...] = o


def _flash_prompt(qkv, *, diff, lam=None, gsub=None, lam_init=0.0, kmean=None):
    b, _, nh, l, hd = qkv.shape
    tq = MOBA_BLOCK
    assert l % FLASH_CHUNK == 0 and FLASH_CHUNK % tq == 0
    rows = 2 * tq if diff else tq
    qkv_specs = [pl.BlockSpec((None, None, None, tq, hd), lambda bi, h, i: (bi, 0, h, i, 0)),
                 pl.BlockSpec((None, None, None, l, hd), lambda bi, h, i: (bi, 1, h, 0, 0)),
                 pl.BlockSpec((None, None, None, l, hd), lambda bi, h, i: (bi, 2, h, 0, 0))]
    if diff:
        extra = [lam, gsub.reshape(1, hd)]
        extra_specs = [pl.BlockSpec(lam.shape, lambda bi, h, i: (0, 0)),
                       pl.BlockSpec((1, hd), lambda bi, h, i: (0, 0))]
    else:
        extra = [kmean]
        extra_specs = [pl.BlockSpec((None, None, kmean.shape[2], hd), lambda bi, h, i: (bi, h, 0, 0))]
    return pl.pallas_call(
        functools.partial(_flash_prompt_kernel, diff=diff, lam_init=lam_init, tq=tq),
        grid=(b, nh, l // tq),
        in_specs=extra_specs + qkv_specs,
        out_specs=pl.BlockSpec((None, None, tq, hd), lambda bi, h, i: (bi, h, i, 0)),
        out_shape=jax.ShapeDtypeStruct((b, nh, l, hd), F32),
        scratch_shapes=[pltpu.VMEM((rows, hd), BF16), pltpu.VMEM((rows, 1), F32),
                        pltpu.VMEM((rows, 1), F32), pltpu.VMEM((rows, hd), F32)],
        compiler_params=_params("arbitrary", "arbitrary", "arbitrary"),
        name="flash_diff" if diff else "flash_moba",
    )(*extra, qkv, qkv, qkv)


def _block_mean_kernel(k_ref, o_ref):
    l, hd = k_ref.shape
    o_ref[...] = jnp.sum(k_ref[...].reshape(l // MOBA_BLOCK, MOBA_BLOCK, hd), axis=1) * (1.0 / MOBA_BLOCK)


def _block_mean_prompt(qkv):
    b, _, nh, l, hd = qkv.shape
    nb = l // MOBA_BLOCK
    return pl.pallas_call(
        _block_mean_kernel,
        grid=(b, nh),
        in_specs=[pl.BlockSpec((None, None, None, l, hd), lambda bi, h: (bi, 1, h, 0, 0))],
        out_specs=pl.BlockSpec((None, None, nb, hd), lambda bi, h: (bi, h, 0, 0)),
        out_shape=jax.ShapeDtypeStruct((b, nh, nb, hd), F32),
        compiler_params=_params("arbitrary", "arbitrary"),
        name="block_mean",
    )(qkv)


def _decode_kernel(*refs, mode, n_pg, paged, lam_init):
    refs = list(refs)
    if paged:
        refs.pop(0)
    q_ref = refs.pop(0)
    pg_refs = [refs.pop(0) for _ in range(n_pg)]
    kn_ref, vn_ref = refs.pop(0), refs.pop(0)
    bias_ref, bnew_ref = refs.pop(0), refs.pop(0)
    km_ref = refs.pop(0) if mode == "moba" else None
    if mode == "diff":
        lam_ref, gsub_ref = refs.pop(0), refs.pop(0)
    o_ref = refs.pop(0)
    lse_ref = refs.pop(0) if mode == "win" else None
    q_s, kf_s, vf_s, m_s, l_s, acc_s = refs[:6]
    sel_s = refs[6] if mode == "moba" else None

    s_id = pl.program_id(1)
    nh, n, hd = q_ref.shape
    rows = q_s.shape[0]
    half = rows // 2
    hpb = nh // 2
    tk = pg_refs[0].shape[0]
    scale = (DIFF_HALF if mode == "diff" else HEAD_DIM) ** -0.5

    @pl.when(s_id == 0)
    def _():
        lane = lax.broadcasted_iota(jnp.int32, (n, hd), 1)
        pieces = []
        for h in range(nh):
            qh = q_ref[h] * scale
            if mode == "diff":
                pieces += [jnp.where(lane < DIFF_HALF, qh, 0.0), jnp.where(lane >= DIFF_HALF, qh, 0.0)]
            else:
                pieces.append(qh)
        q_s[...] = jnp.concatenate(pieces, axis=0).astype(BF16)
        m_s[...] = jnp.full(m_s.shape, NEG, F32)
        l_s[...] = jnp.zeros(l_s.shape, F32)
        acc_s[...] = jnp.zeros(acc_s.shape, F32)
        if mode == "moba":
            n_blk = km_ref.shape[0] // nh
            for h in range(nh):
                gate = _dot_nt_3pass(q_ref[h], km_ref[pl.ds(h, n_blk, stride=nh), :])
                sel_s[h * n:(h + 1) * n, :] = _topk_bias(gate, n_blk, MOBA_TOPK)

    for hb in range(2):
        hs = slice(hb * hpb, (hb + 1) * hpb)
        rs = slice(hb * half, (hb + 1) * half)
        for r in range(n_pg):
            dst = slice(r * tk * hpb, (r + 1) * tk * hpb)
            kf_s[dst, :] = pg_refs[r][:, 0, hs, :].reshape(tk * hpb, hd).astype(BF16)
            vf_s[dst, :] = pg_refs[r][:, 1, hs, :].reshape(tk * hpb, hd).astype(BF16)
        s = _dot_nt(q_s[rs, :], kf_s[...]) + bias_ref[...]
        if mode == "moba":
            pages_per_blk = MOBA_BLOCK // tk
            cols = []
            for r in range(n_pg // pages_per_blk):
                col = _lane_column(sel_s[rs, :], s_id * (n_pg // pages_per_blk) + r)
                cols.append(jnp.broadcast_to(col, (half, MOBA_BLOCK * hpb)))
            s = s + jnp.concatenate(cols, axis=1)
        _online_update(s, vf_s[...], m_s.at[rs, :], l_s.at[rs, :], acc_s.at[rs, :])

    @pl.when(s_id == pl.num_programs(1) - 1)
    def _():
        pad = jnp.zeros((LANES - hpb * n, hd), F32)
        for hb in range(2):
            hs = slice(hb * hpb, (hb + 1) * hpb)
            rs = slice(hb * half, (hb + 1) * half)
            kn = jnp.concatenate([kn_ref[hs].reshape(hpb * n, hd), pad], axis=0).astype(BF16)
            vn = jnp.concatenate([vn_ref[hs].reshape(hpb * n, hd), pad], axis=0).astype(BF16)
            _online_update(_dot_nt(q_s[rs, :], kn) + bnew_ref[...], vn,
                           m_s.at[rs, :], l_s.at[rs, :], acc_s.at[rs, :])
        lsum = l_s[...]
        o = acc_s[...] * (1.0 / lsum)
        if mode == "diff":
            lam_full = _diff_lambda(lam_ref[...], lam_init)
            for h in range(nh):
                oh = o[2 * h * n:(2 * h + 1) * n] - lam_full * o[(2 * h + 1) * n:(2 * h + 2) * n]
                o_ref[h] = _sub_norm(oh, gsub_ref[...], lam_init)
        else:
            o_ref[...] = o.reshape(nh, n, hd)
        if mode == "win":
            lse_ref[...] = jnp.broadcast_to(m_s[...] + jnp.log(lsum), (rows, LANES)).reshape(nh, n, LANES)


def _head_match_bias(half_rows, rows_per_head, n_keys, hpb):
    r = np.arange(half_rows)[:, None] // rows_per_head
    c = np.arange(n_keys * hpb)[None, :] % hpb
    return np.where(r == c, 0.0, NEG)


def _decode_attn(qkv, gt, cache, layer, *, mode, page_table=None, dil=1, window=None,
                 kmean=None, lam=None, gsub=None, lam_init=0.0):
    b, _, nh, n, hd = qkv.shape
    hpb = nh // 2
    comps = 2 if mode == "diff" else 1
    rows = nh * comps * n
    half = rows // 2
    paged = page_table is not None
    tk = PAGE_SIZE
    total_pages = page_table.shape[1] if paged else cache.shape[2] // tk
    n_pg = min(PAGES_PER_STEP, total_pages)
    steps = total_pages // n_pg
    blk = (None, None, tk, 2, nh, hd)
    if paged:
        pg_specs = [pl.BlockSpec(blk, functools.partial(
            lambda bi, s, pt, r: (layer, pt[bi, s * n_pg + r], 0, 0, 0, 0), r=r)) for r in range(n_pg)]
    else:
        pg_specs = [pl.BlockSpec(blk, functools.partial(
            lambda bi, s, r: (layer, bi, s * n_pg + r, 0, 0, 0), r=r)) for r in range(n_pg)]

    def fixed(shape, idx):
        if paged:
            return pl.BlockSpec(shape, lambda bi, s, pt: idx(bi, s))
        return pl.BlockSpec(shape, lambda bi, s: idx(bi, s))

    qi = np.arange(half)[:, None] % n
    bias = _head_match_bias(half, comps * n, total_pages * tk if mode == "win" else n_pg * tk, hpb)
    cn = np.arange(LANES)[None, :]
    new_ok = (cn < hpb * n) & (cn // n == np.arange(half)[:, None] // (comps * n)) & (cn % n <= qi)
    if mode == "win":
        n_buf = total_pages * tk
        dist = n_buf + qi - np.arange(n_buf * hpb)[None, :] // hpb
        bias = np.where((dist % dil == 0) & (dist <= window), bias, NEG)
        new_ok &= (qi - cn % n) % dil == 0
    bias = jnp.asarray(bias, F32)
    bias_new = jnp.asarray(np.where(new_ok, 0.0, NEG), F32)

    head_blk = (None, None, nh, n, hd)
    args = [qkv] + [cache] * n_pg + [qkv, qkv, bias, bias_new]
    specs = [fixed(head_blk, lambda bi, s: (bi, gt, 0, 0, 0))] + pg_specs + [
        fixed(head_blk, lambda bi, s: (bi, gt + 1, 0, 0, 0)),
        fixed(head_blk, lambda bi, s: (bi, gt + 2, 0, 0, 0)),
        fixed((half, n_pg * tk * hpb), (lambda bi, s: (0, s)) if mode == "win" else (lambda bi, s: (0, 0))),
        fixed((half, LANES), lambda bi, s: (0, 0))]
    if mode == "moba":
        n_blk = kmean.shape[1]
        args.append(kmean.reshape(b, n_blk * nh, hd))
        specs.append(fixed((None, n_blk * nh, hd), lambda bi, s: (bi, 0, 0)))
    if mode == "diff":
        args += [lam, gsub.reshape(1, hd)]
        specs += [fixed(lam.shape, lambda bi, s: (0, 0)), fixed((1, hd), lambda bi, s: (0, 0))]
    scratch = [pltpu.VMEM((rows, hd), BF16),
               pltpu.VMEM((n_pg * tk * hpb, hd), BF16), pltpu.VMEM((n_pg * tk * hpb, hd), BF16),
               pltpu.VMEM((rows, 1), F32), pltpu.VMEM((rows, 1), F32), pltpu.VMEM((rows, hd), F32)]
    if mode == "moba":
        scratch.append(pltpu.VMEM((rows, n_blk), F32))
    out_blk = (None, nh, n, hd)
    out_shape = [jax.ShapeDtypeStruct((b, nh, n, hd), F32)]
    out_specs = [fixed(out_blk, lambda bi, s: (bi, 0, 0, 0))]
    if mode == "win":
        out_shape.append(jax.ShapeDtypeStruct((b, nh, n, LANES), F32))
        out_specs.append(fixed(out_blk, lambda bi, s: (bi, 0, 0, 0)))
    kern = functools.partial(_decode_kernel, mode=mode, n_pg=n_pg, paged=paged, lam_init=lam_init)
    name = "decode_" + mode
    if paged:
        grid_spec = pltpu.PrefetchScalarGridSpec(num_scalar_prefetch=1, grid=(b, steps), in_specs=specs,
                                                 out_specs=out_specs, scratch_shapes=scratch)
        out = pl.pallas_call(kern, grid_spec=grid_spec, out_shape=out_shape, name=name,
                             compiler_params=_params("arbitrary", "arbitrary"))(page_table, *args)
    else:
        out = pl.pallas_call(kern, grid=(b, steps), in_specs=specs, out_specs=out_specs, out_shape=out_shape,
                             scratch_shapes=scratch, name=name,
                             compiler_params=_params("arbitrary", "arbitrary"))(*args)
    return out if mode == "win" else out[0]


def _merge_groups_kernel(*refs):
    o_refs, lse_refs, out_ref = refs[:N_GROUPS], refs[N_GROUPS:2 * N_GROUPS], refs[2 * N_GROUPS]
    lses = [r[...] for r in lse_refs]
    m = jnp.maximum(jnp.maximum(lses[0], lses[1]), lses[2])
    es = [jnp.exp(v - m) for v in lses]
    inv = 1.0 / (es[0] + es[1] + es[2])
    out_ref[...] = sum(es[g] * o_refs[g][...] for g in range(N_GROUPS)) * inv


def _merge_groups(outs, lses):
    shape = outs[0].shape
    spec = pl.BlockSpec(shape, lambda i: (0,) * len(shape))
    return pl.pallas_call(
        _merge_groups_kernel, grid=(1,), in_specs=[spec] * (2 * N_GROUPS), out_specs=spec,
        out_shape=jax.ShapeDtypeStruct(shape, F32), compiler_params=_params("arbitrary"),
        name="merge_groups",
    )(*outs, *lses)


def _paged_mean_kernel(pt_ref, *refs, n_pages, pages_per_blk):
    o_ref = refs[n_pages]
    for j in range(n_pages // pages_per_blk):
        tot = sum(jnp.sum(refs[j * pages_per_blk + r][...], axis=0) for r in range(pages_per_blk))
        o_ref[j] = tot * (1.0 / MOBA_BLOCK)


def _paged_block_mean(cache, layer, page_table):
    b, total_pages = page_table.shape
    tk, _, nh, hd = cache.shape[2:]
    pages_per_blk = MOBA_BLOCK // tk
    n_pages = 8
    steps = total_pages // n_pages
    specs = [pl.BlockSpec((None, None, tk, None, nh, hd), functools.partial(
        lambda bi, s, pt, r: (layer, pt[bi, s * n_pages + r], 0, 0, 0, 0), r=r)) for r in range(n_pages)]
    return pl.pallas_call(
        functools.partial(_paged_mean_kernel, n_pages=n_pages, pages_per_blk=pages_per_blk),
        grid_spec=pltpu.PrefetchScalarGridSpec(
            num_scalar_prefetch=1, grid=(b, steps), in_specs=specs,
            out_specs=pl.BlockSpec((None, n_pages // pages_per_blk, nh, hd), lambda bi, s, pt: (bi, s, 0, 0))),
        out_shape=jax.ShapeDtypeStruct((b, total_pages // pages_per_blk, nh, hd), F32),
        compiler_params=_params("arbitrary", "arbitrary"),
        name="paged_block_mean",
    )(page_table, *([cache] * n_pages))


def _rope_tables(pos, half, reps=1):
    inv = ROPE_THETA ** (-np.arange(0, 2 * half, 2, dtype=np.float64) / (2 * half))
    ang = np.asarray(pos, np.float64)[:, None] * inv[None, :]
    cos = np.concatenate([np.cos(ang), np.cos(ang)], axis=1)
    sin = np.concatenate([-np.sin(ang), np.sin(ang)], axis=1)
    tile = LANES // (2 * half)
    cos = np.tile(np.tile(cos, (1, tile)), (reps, 1))
    sin = np.tile(np.tile(sin, (1, tile)), (reps, 1))
    return jnp.asarray(cos, F32), jnp.asarray(sin, F32)


def _rows_of(kv_layers):
    return jnp.transpose(jnp.stack(kv_layers), (0, 1, 4, 2, 3, 5))


def kernel(x_prompt, x_sample, c_prompt, c_sample, cache_win1_kv, cache_win2_kv, cache_win3_kv, cache_kv,
           page_table, norm1_g, norm2_g, w_mod, b_mod, w_qkv_a, w_o_a, w_qkv_b, w_o_b, lambda_b, subln_g_b,
           w_qkv_c, w_o_c, w_up, w_down, final_g):
    bp, seq, d = x_prompt.shape
    bs, n_new, _ = x_sample.shape
    depth = w_mod.shape[0]
    past_len = page_table.shape[1] * PAGE_SIZE
    assert past_len % MOBA_BLOCK == 0 and seq % WIN_TILE == 0 and n_new == SUBLANES
    assert cache_kv.shape[2] == PAGE_SIZE
    pos_p = np.arange(seq)
    pos_s = past_len + np.arange(n_new)
    tl_p = 1024

    rope_p = {h: _rope_tables(pos_p, h) for h in (HEAD_DIM // 2, DIFF_HALF // 2)}
    rope_s = {h: _rope_tables(pos_s, h, reps=bs) for h in (HEAD_DIM // 2, DIFF_HALF // 2)}

    rows_c = 2 * SUBLANES
    c_all = jnp.concatenate([c_prompt, c_sample, jnp.zeros((rows_c - bp - bs, d), F32)], axis=0)
    mods = _modulation(c_all, w_mod, b_mod)
    win_caches = (cache_win1_kv, cache_win2_kv, cache_win3_kv)

    xp, xs = x_prompt, x_sample
    win_p = [[] for _ in range(N_GROUPS)]
    win_s = [[] for _ in range(N_GROUPS)]
    kv_p, kv_s = [], []
    for i in range(depth):
        kind = i % N_MIXERS
        j = i // N_MIXERS
        mp = [mods[i, :bp, t * d:(t + 1) * d][:, None, :] for t in range(6)]
        ms = [mods[i, bp:bp + bs, t * d:(t + 1) * d][:, None, :] for t in range(6)]
        if kind == 0:
            half = HEAD_DIM // 2
            qkv_p = _qkv(xp, norm1_g[i], mp[1], mp[0], *rope_p[half], w_qkv_a[j], half=half, tl=tl_p)
            qkv_s = _qkv(xs, norm1_g[i], ms[1], ms[0], *rope_s[half], w_qkv_a[j], half=half, tl=n_new)
            o_p = _win_prompt(qkv_p)
            outs_s, lses_s = [], []
            for g, (window, dil) in enumerate(DIL_GROUPS):
                keep = min(window, seq)
                win_p[g].append(qkv_p[:, 3 * g + 1:3 * g + 3, :, seq - keep:])
                o_g, lse_g = _decode_attn(qkv_s, 3 * g, win_caches[g], j, mode="win", dil=dil, window=window)
                outs_s.append(o_g)
                lses_s.append(lse_g)
                win_s[g].append(qkv_s[:, 3 * g + 1:3 * g + 3])
            o_s = _merge_groups(outs_s, lses_s)
            w_o = w_o_a[j]
        else:
            slot = i - (i + 2) // N_MIXERS
            if kind == 1:
                half = DIFF_HALF // 2
                w_qkv, w_o = w_qkv_b[j], w_o_b[j]
            else:
                half = HEAD_DIM // 2
                w_qkv, w_o = w_qkv_c[j], w_o_c[j]
            qkv_p = _qkv(xp, norm1_g[i], mp[1], mp[0], *rope_p[half], w_qkv, half=half, tl=tl_p)
            qkv_s = _qkv(xs, norm1_g[i], ms[1], ms[0], *rope_s[half], w_qkv, half=half, tl=n_new)
            if kind == 1:
                lam_init = 0.8 - 0.6 * math.exp(-0.3 * i)
                o_p = _flash_prompt(qkv_p, diff=True, lam=lambda_b[j], gsub=subln_g_b[j], lam_init=lam_init)
                o_s = _decode_attn(qkv_s, 0, cache_kv, slot, mode="diff", page_table=page_table,
                                   lam=lambda_b[j], gsub=subln_g_b[j], lam_init=lam_init)
            else:
                o_p = _flash_prompt(qkv_p, diff=False, kmean=_block_mean_prompt(qkv_p))
                o_s = _decode_attn(qkv_s, 0, cache_kv, slot, mode="moba", page_table=page_table,
                                   kmean=_paged_block_mean(cache_kv, slot, page_table))
            kv_p.append(qkv_p[:, 1:3])
            kv_s.append(qkv_s[:, 1:3])
        xp = _proj(o_p, w_o, mp[2], xp, tl=tl_p)
        xs = _proj(o_s, w_o, ms[2], xs, tl=n_new)
        xp = _mlp(xp, norm2_g[i], mp[4], mp[3], mp[5], w_up[i], w_down[i], tl=tl_p)
        xs = _mlp(xs, norm2_g[i], ms[4], ms[3], ms[5], w_up[i], w_down[i], tl=n_new)
    y_prompt = _final_norm(xp, final_g)
    y_sample = _final_norm(xs, final_g)
    new_win_s = [jnp.concatenate([win_caches[g][:, :, n_new:], _rows_of(win_s[g])], axis=2)
                 for g in range(N_GROUPS)]
    return (y_prompt, y_sample,
            _rows_of(win_p[0]), _rows_of(win_p[1]), _rows_of(win_p[2]), _rows_of(kv_p),
            new_win_s[0], new_win_s[1], new_win_s[2], _rows_of(kv_s))
```

```python
import functools
import math

import numpy as np
import jax
import jax.numpy as jnp
from jax import lax
from jax.experimental import pallas as pl
from jax.experimental.pallas import tpu as pltpu

F32 = jnp.float32
BF16 = jnp.bfloat16

HEAD_DIM = 128
DIL_GROUPS = ((128, 1), (512, 4), (2048, 16))
N_GROUPS = len(DIL_GROUPS)
DIFF_HALF = HEAD_DIM // 2
MOBA_BLOCK = 256
MOBA_TOPK = 3
PAGE_SIZE = 128
ROPE_THETA = 10000.0
NORM_EPS = 1e-6
SUBLN_EPS = 1e-5
N_MIXERS = 3

LANES = 128
SUBLANES = 8
NEG = -1e30
VMEM_LIMIT = 56 * 1024 * 1024
PAGES_PER_STEP = 4
ROW_CHUNK = 128
WIN_TILE = 2048
WIN_UNROLL = 4
FLASH_CHUNK = 1024
FLASH_ROWS = 128


def _params(*sem):
    return pltpu.CompilerParams(dimension_semantics=sem, vmem_limit_bytes=VMEM_LIMIT)


def _dot(a, b):
    return jnp.dot(a, b, preferred_element_type=F32)


def _dot_nt(a, b):
    return lax.dot_general(a, b, (((1,), (1,)), ((), ())), preferred_element_type=F32)


def _split_bf16(x):
    hi = x.astype(BF16)
    lo = (x - hi.astype(F32)).astype(BF16)
    return hi, lo


def _dot_nt_3pass(a, b):
    ah, al = _split_bf16(a)
    bh, bl = _split_bf16(b)
    return _dot_nt(ah, bh) + _dot_nt(ah, bl) + _dot_nt(al, bh)


def _norm_mod(x, g, scale, shift):
    y = x * lax.rsqrt(jnp.mean(x * x, axis=-1, keepdims=True) + NORM_EPS) * g
    return y * (1.0 + scale) + shift


def _for_row_chunks(bb, tl, body):
    if bb > 1 or tl <= ROW_CHUNK:
        body(slice(None))
        return

    def step(c, carry):
        body(pl.ds(pl.multiple_of(c * ROW_CHUNK, ROW_CHUNK), ROW_CHUNK))
        return carry

    lax.fori_loop(0, tl // ROW_CHUNK, step, 0)


def _store_norm_mod(x_ref, g_ref, sc_ref, sh_ref, h_s):
    bb, tl, d = x_ref.shape

    def body(rows):
        h = _norm_mod(x_ref[:, rows, :], g_ref[...], sc_ref[...], sh_ref[...])
        h_s[rows, :] = h.reshape(-1, d).astype(BF16)

    _for_row_chunks(bb, tl, body)


def _batch_block(b, l, tl):
    return b if tl == l and b * l <= 1024 else 1


def _mod_kernel(c_ref, w_ref, b_ref, o_ref):
    c = c_ref[...]
    a = (c / (1.0 + jnp.exp(-c))).astype(BF16)
    o_ref[...] = _dot(a, w_ref[...].astype(BF16)) + b_ref[...]


def _modulation(c_all, w_mod, b_mod):
    depth, d, n = w_mod.shape
    rows = c_all.shape[0]
    tn = 1024
    return pl.pallas_call(
        _mod_kernel,
        grid=(depth, n // tn),
        in_specs=[pl.BlockSpec((rows, d), lambda i, j: (0, 0)),
                  pl.BlockSpec((None, d, tn), lambda i, j: (i, 0, j)),
                  pl.BlockSpec((None, 1, tn), lambda i, j: (i, 0, j))],
        out_specs=pl.BlockSpec((None, rows, tn), lambda i, j: (i, 0, j)),
        out_shape=jax.ShapeDtypeStruct((depth, rows, n), F32),
        compiler_params=_params("arbitrary", "arbitrary"),
        name="modulation",
    )(c_all, w_mod, b_mod.reshape(depth, 1, n))


def _rope_chunk(x, cos, sin, half):
    if 2 * half == LANES:
        rot = pltpu.roll(x, half, 1)
    else:
        lane = lax.broadcasted_iota(jnp.int32, x.shape, 1)
        rot = jnp.where(lane % (2 * half) < half, pltpu.roll(x, LANES - half, 1), pltpu.roll(x, half, 1))
    return x * cos + rot * sin


def _qkv_kernel(x_ref, g_ref, sc_ref, sh_ref, cos_ref, sin_ref, w_ref, o_ref, h_s, *, half):
    n = pl.program_id(2)
    bb, tl, d = x_ref.shape
    tn = w_ref.shape[1]

    @pl.when(n == 0)
    def _():
        _store_norm_mod(x_ref, g_ref, sc_ref, sh_ref, h_s)

    acc = _dot(h_s[...], w_ref[...].astype(BF16))
    cos = cos_ref[...]
    sin = sin_ref[...]
    for c in range(tn // LANES):
        chunk = _rope_chunk(acc[:, c * LANES:(c + 1) * LANES], cos, sin, half)
        o_ref[:, c, :, :] = chunk.reshape(bb, tl, LANES)


def _qkv(x, g, scale, shift, cos, sin, w, *, half, tl):
    b, l, d = x.shape
    n = w.shape[1]
    nh = d // HEAD_DIM
    tn = 512
    hpt = tn // HEAD_DIM
    tpk = d // tn
    bb = _batch_block(b, l, tl)
    table = pl.BlockSpec((None, bb * tl, LANES), lambda i, j, k: (((k // tpk) % 3) // 2, j, 0))
    return pl.pallas_call(
        functools.partial(_qkv_kernel, half=half),
        grid=(b // bb, l // tl, n // tn),
        in_specs=[pl.BlockSpec((bb, tl, d), lambda i, j, k: (i, j, 0)),
                  pl.BlockSpec((1, d), lambda i, j, k: (0, 0)),
                  pl.BlockSpec((bb, 1, d), lambda i, j, k: (i, 0, 0)),
                  pl.BlockSpec((bb, 1, d), lambda i, j, k: (i, 0, 0)),
                  table, table,
                  pl.BlockSpec((d, tn), lambda i, j, k: (0, k))],
        out_specs=pl.BlockSpec((bb, None, hpt, tl, HEAD_DIM), lambda i, j, k: (i, k // tpk, k % tpk, j, 0)),
        out_shape=jax.ShapeDtypeStruct((b, n // d, nh, l, HEAD_DIM), F32),
        scratch_shapes=[pltpu.VMEM((bb * tl, d), BF16)],
        compiler_params=_params("arbitrary", "arbitrary", "arbitrary"),
        name="qkv",
    )(x, g.reshape(1, d), scale, shift, cos, sin, w)


def _proj_kernel(a_ref, w_ref, gate_ref, res_ref, o_ref, a_s):
    n = pl.program_id(2)
    bb, nh, tl, hd = a_ref.shape
    tn = w_ref.shape[1]

    @pl.when(n == 0)
    def _():
        def body(rows):
            for h in range(nh):
                a_s[rows, h * hd:(h + 1) * hd] = a_ref[:, h, rows, :].reshape(-1, hd).astype(BF16)

        _for_row_chunks(bb, tl, body)

    acc = _dot(a_s[...], w_ref[...].astype(BF16))
    o_ref[...] = res_ref[...] + gate_ref[...] * acc.reshape(bb, tl, tn)


def _proj(a, w, gate, res, *, tl):
    b, l, d = res.shape
    nh = a.shape[1]
    tn = 512
    bb = _batch_block(b, l, tl)
    return pl.pallas_call(
        _proj_kernel,
        grid=(b // bb, l // tl, d // tn),
        in_specs=[pl.BlockSpec((bb, nh, tl, HEAD_DIM), lambda i, j, k: (i, 0, j, 0)),
                  pl.BlockSpec((d, tn), lambda i, j, k: (0, k)),
                  pl.BlockSpec((bb, 1, tn), lambda i, j, k: (i, 0, k)),
                  pl.BlockSpec((bb, tl, tn), lambda i, j, k: (i, j, k))],
        out_specs=pl.BlockSpec((bb, tl, tn), lambda i, j, k: (i, j, k)),
        out_shape=jax.ShapeDtypeStruct((b, l, d), F32),
        scratch_shapes=[pltpu.VMEM((bb * tl, d), BF16)],
        compiler_params=_params("arbitrary", "arbitrary", "arbitrary"),
        name="proj",
    )(a, w, gate, res)


def _mlp_kernel(x_ref, g_ref, sc_ref, sh_ref, gate_ref, wu_ref, wd_ref, o_ref, h_s):
    f = pl.program_id(2)
    bb, tl, d = x_ref.shape
    nchunk = 512

    @pl.when(f == 0)
    def _():
        _store_norm_mod(x_ref, g_ref, sc_ref, sh_ref, h_s)
        o_ref[...] = jnp.zeros_like(o_ref)

    a = jnp.maximum(_dot(h_s[...], wu_ref[...].astype(BF16)), 0.0)
    a = (a * a).astype(BF16)
    for c in range(d // nchunk):
        sl = slice(c * nchunk, (c + 1) * nchunk)
        o_ref[:, :, sl] += _dot(a, wd_ref[:, sl].astype(BF16)).reshape(bb, tl, nchunk)

    @pl.when(f == pl.num_programs(2) - 1)
    def _():
        o_ref[...] = x_ref[...] + gate_ref[...] * o_ref[...]


def _mlp(x, g, scale, shift, gate, w_up, w_down, *, tl):
    b, l, d = x.shape
    ff = w_up.shape[1]
    bb = _batch_block(b, l, tl)
    big = bb * tl >= 1024
    tf = 512
    vec = pl.BlockSpec((bb, 1, d), lambda i, j, k: (i, 0, 0))
    x_mode = dict(pipeline_mode=pl.Buffered(1)) if big else {}
    return pl.pallas_call(
        _mlp_kernel,
        grid=(b // bb, l // tl, ff // tf),
        in_specs=[pl.BlockSpec((bb, tl, d), lambda i, j, k: (i, j, 0), **x_mode),
                  pl.BlockSpec((1, d), lambda i, j, k: (0, 0)),
                  vec, vec, vec,
                  pl.BlockSpec((d, tf), lambda i, j, k: (0, k)),
                  pl.BlockSpec((tf, d), lambda i, j, k: (k, 0))],
        out_specs=pl.BlockSpec((bb, tl, d), lambda i, j, k: (i, j, 0), **x_mode),
        out_shape=jax.ShapeDtypeStruct((b, l, d), F32),
        scratch_shapes=[pltpu.VMEM((bb * tl, d), BF16)],
        compiler_params=_params("arbitrary", "arbitrary", "arbitrary"),
        name="mlp",
    )(x, g.reshape(1, d), scale, shift, gate, w_up, w_down)


def _final_norm_kernel(x_ref, g_ref, o_ref):
    x = x_ref[...]
    o_ref[...] = x * lax.rsqrt(jnp.mean(x * x, axis=-1, keepdims=True) + NORM_EPS) * g_ref[...]


def _final_norm(x, g):
    b, l, d = x.shape
    tl = min(l, 512)
    return pl.pallas_call(
        _final_norm_kernel,
        grid=(b, l // tl),
        in_specs=[pl.BlockSpec((1, tl, d), lambda i, j: (i, j, 0)),
                  pl.BlockSpec((1, d), lambda i, j: (0, 0))],
        out_specs=pl.BlockSpec((1, tl, d), lambda i, j: (i, j, 0)),
        out_shape=jax.ShapeDtypeStruct((b, l, d), F32),
        compiler_params=_params("arbitrary", "arbitrary"),
        name="final_norm",
    )(x, g.reshape(1, d))


def _online_update(s, v, m_s, l_s, acc_s):
    m_prev = m_s[...]
    m_new = jnp.maximum(m_prev, jnp.max(s, axis=1, keepdims=True))
    alpha = jnp.exp(m_prev - m_new)
    p = jnp.exp(s - m_new)
    l_s[...] = alpha * l_s[...] + jnp.sum(p, axis=1, keepdims=True)
    acc_s[...] = alpha * acc_s[...] + _dot(p.astype(BF16), v)
    m_s[...] = m_new


def _lane_column(x, j):
    lane = lax.broadcasted_iota(jnp.int32, x.shape, 1)
    return jnp.sum(jnp.where(lane == j, x, 0.0), axis=1, keepdims=True)


def _topk_bias(gate, n_valid, k):
    lane = lax.broadcasted_iota(jnp.int32, gate.shape, 1).astype(F32)
    g = jnp.where(lane < n_valid, gate, NEG)
    bias = jnp.full(gate.shape, NEG, F32)
    for _ in range(k):
        mx = jnp.max(g, axis=1, keepdims=True)
        first = jnp.min(jnp.where(g == mx, lane, float(gate.shape[1])), axis=1, keepdims=True)
        pick = (lane == first) & (mx > 0.5 * NEG)
        bias = jnp.where(pick, 0.0, bias)
        g = jnp.where(lane == first, NEG, g)
    return bias


def _diff_lambda(lam, lam_init):
    return (jnp.exp(jnp.sum(lam[0:1] * lam[1:2], axis=1, keepdims=True))
            - jnp.exp(jnp.sum(lam[2:3] * lam[3:4], axis=1, keepdims=True)) + lam_init)


def _sub_norm(o, gsub, lam_init):
    return o * lax.rsqrt(jnp.mean(o * o, axis=-1, keepdims=True) + SUBLN_EPS) * gsub * (1.0 - lam_init)


def _win_prompt_kernel(*refs, nw):
    in_refs, (o_ref, kcat_s, vcat_s, og_s, lse_s) = refs[:5 * N_GROUPS], refs[5 * N_GROUPS:]
    t = pl.program_id(2)
    tile = o_ref.shape[0]
    scale = HEAD_DIM ** -0.5
    r = lax.broadcasted_iota(jnp.int32, (nw, 2 * nw), 0)
    c = lax.broadcasted_iota(jnp.int32, (nw, 2 * nw), 1)
    dist = nw + r - c
    band = (dist >= 0) & (dist <= nw)

    for g, (window, dil) in enumerate(DIL_GROUPS):
        q_ref, kc_ref, kp_ref, vc_ref, vp_ref = in_refs[5 * g:5 * g + 5]
        kcat_s[0:window, :] = kp_ref[...]
        kcat_s[window:window + tile, :] = kc_ref[...]
        vcat_s[0:window, :] = vp_ref[...]
        vcat_s[window:window + tile, :] = vc_ref[...]

        def blocks(ug, carry, g=g, dil=dil, window=window, q_ref=q_ref):
            work = []
            for i in range(WIN_UNROLL):
                u = ug * WIN_UNROLL + i
                start = u % dil + (u // dil) * window
                if dil == 1:
                    rows_q = pl.ds(start, nw)
                    rows_k = pl.ds(start, 2 * nw)
                else:
                    rows_q = pl.ds(start, nw, stride=dil)
                    rows_k = pl.ds(start, 2 * nw, stride=dil)
                q = (q_ref[rows_q, :] * scale).astype(BF16)
                k = kcat_s[rows_k, :].astype(BF16)
                first_key = jnp.where((t > 0) | (u >= dil), 0, nw)
                s = _dot_nt(q, k) + jnp.where(band & (c >= first_key), 0.0, NEG)
                work.append((rows_q, rows_k, s))
            for rows_q, rows_k, s in work:
                v = vcat_s[rows_k, :].astype(BF16)
                m = jnp.max(s, axis=1, keepdims=True)
                p = jnp.exp(s - m)
                l = jnp.sum(p, axis=1, keepdims=True)
                og_s[g, rows_q, :] = _dot(p.astype(BF16), v) * (1.0 / l)
                lse_s[g, rows_q, :] = jnp.broadcast_to(m + jnp.log(l), (nw, LANES))
            return carry

        lax.fori_loop(0, tile // nw // WIN_UNROLL, blocks, 0)

    def merge(rows):
        lses = [lse_s[g, rows, :] for g in range(N_GROUPS)]
        m = jnp.maximum(jnp.maximum(lses[0], lses[1]), lses[2])
        es = [jnp.exp(v - m) for v in lses]
        inv = 1.0 / (es[0] + es[1] + es[2])
        o_ref[rows, :] = sum(es[g] * og_s[g, rows, :] for g in range(N_GROUPS)) * inv

    _for_row_chunks(1, tile, merge)


def _win_prompt(qkv):
    b, _, nh, l, hd = qkv.shape
    tile = WIN_TILE
    nw = DIL_GROUPS[0][0] // DIL_GROUPS[0][1]
    specs = []
    for g, (window, dil) in enumerate(DIL_GROUPS):
        assert window // dil == nw and tile % window == 0

        def cur(t, g=g):
            return pl.BlockSpec((None, None, None, tile, hd), lambda bi, h, i: (bi, 3 * g + t, h, i, 0))

        def prev(t, g=g, per=tile // window, window=window):
            return pl.BlockSpec((None, None, None, window, hd),
                                lambda bi, h, i: (bi, 3 * g + t, h, jnp.maximum(i * per - 1, 0), 0))

        specs += [cur(0), cur(1), prev(1), cur(2), prev(2)]
    max_w = DIL_GROUPS[-1][0]
    return pl.pallas_call(
        functools.partial(_win_prompt_kernel, nw=nw),
        grid=(b, nh, l // tile),
        in_specs=specs,
        out_specs=pl.BlockSpec((None, None, tile, hd), lambda bi, h, i: (bi, h, i, 0)),
        out_shape=jax.ShapeDtypeStruct((b, nh, l, hd), F32),
        scratch_shapes=[pltpu.VMEM((max_w + tile, hd), F32), pltpu.VMEM((max_w + tile, hd), F32),
                        pltpu.VMEM((N_GROUPS, tile, hd), F32), pltpu.VMEM((N_GROUPS, tile, LANES), F32)],
        compiler_params=_params("arbitrary", "arbitrary", "arbitrary"),
        name="win_prompt",
    )(*([qkv] * (5 * N_GROUPS)))


def _flash_prompt_kernel(*refs, diff, lam_init, tq):
    if diff:
        lam_ref, gsub_ref, q_ref, k_ref, v_ref, o_ref, qs_s, m_s, l_s, acc_s = refs
    else:
        km_ref, q_ref, k_ref, v_ref, o_ref, qs_s, m_s, l_s, acc_s = refs
    qi = pl.program_id(2)
    rows = qs_s.shape[0]
    lane = lax.broadcasted_iota(jnp.int32, (tq, HEAD_DIM), 1)
    q = q_ref[...]
    if diff:
        qsc = q * (DIFF_HALF ** -0.5)
        qs_s[...] = jnp.concatenate([jnp.where(lane < DIFF_HALF, qsc, 0.0),
                                     jnp.where(lane >= DIFF_HALF, qsc, 0.0)], axis=0).astype(BF16)
    else:
        qs_s[...] = (q * (HEAD_DIM ** -0.5)).astype(BF16)
        own = (qi * tq + lax.broadcasted_iota(jnp.int32, (tq, 1), 0)) // MOBA_BLOCK
        sel_bias = _topk_bias(_dot_nt_3pass(q, km_ref[...]), own.astype(F32), MOBA_TOPK)
    m_s[...] = jnp.full(m_s.shape, NEG, F32)
    l_s[...] = jnp.zeros(l_s.shape, F32)
    acc_s[...] = jnp.zeros(acc_s.shape, F32)

    chunk = FLASH_CHUNK
    diag = qi // (chunk // tq)

    def selection_bias(ci):
        per_chunk = chunk // MOBA_BLOCK
        cols = [jnp.broadcast_to(_lane_column(sel_bias, ci * per_chunk + t), (tq, MOBA_BLOCK))
                for t in range(per_chunk)]
        return jnp.concatenate(cols, axis=1)

    def consume(ci, bias):
        rows_k = pl.ds(pl.multiple_of(ci * chunk, chunk), chunk)
        k = k_ref[rows_k, :].astype(BF16)
        v = v_ref[rows_k, :].astype(BF16)
        sub = FLASH_ROWS
        scores = [_dot_nt(qs_s[r0:r0 + sub, :], k) for r0 in range(0, rows, sub)]
        for r0, s in zip(range(0, rows, sub), scores):
            rs = slice(r0, r0 + sub)
            if bias is not None:
                s = s + bias[r0 % tq:r0 % tq + sub]
            _online_update(s, v, m_s.at[rs, :], l_s.at[rs, :], acc_s.at[rs, :])

    def past_chunk(ci, carry):
        consume(ci, None if diff else selection_bias(ci))
        return carry

    lax.fori_loop(0, diag, past_chunk, 0)

    q_pos = qi * tq + lax.broadcasted_iota(jnp.int32, (tq, chunk), 0)
    k_pos = diag * chunk + lax.broadcasted_iota(jnp.int32, (tq, chunk), 1)
    bias = jnp.where(k_pos <= q_pos, 0.0, NEG)
    if not diff:
        bias = jnp.where(k_pos // MOBA_BLOCK >= q_pos // MOBA_BLOCK, bias, selection_bias(diag))
    consume(diag, bias)

    o = acc_s[...] * (1.0 / l_s[...])
    if diff:
        o = o[:tq] - _diff_lambda(lam_ref[...], lam_init) * o[tq:]
        o = _sub_norm(o, gsub_ref[...], lam_init)
    o_ref[...] = o


def _flash_prompt(qkv, *, diff, lam=None, gsub=None, lam_init=0.0, kmean=None):
    b, _, nh, l, hd = qkv.shape
    tq = MOBA_BLOCK if diff else 2 * MOBA_BLOCK
    assert l % FLASH_CHUNK == 0 and FLASH_CHUNK % tq == 0
    rows = 2 * tq if diff else tq
    qkv_specs = [pl.BlockSpec((None, None, None, tq, hd), lambda bi, h, i: (bi, 0, h, i, 0)),
                 pl.BlockSpec((None, None, None, l, hd), lambda bi, h, i: (bi, 1, h, 0, 0)),
                 pl.BlockSpec((None, None, None, l, hd), lambda bi, h, i: (bi, 2, h, 0, 0))]
    if diff:
        extra = [lam, gsub.reshape(1, hd)]
        extra_specs = [pl.BlockSpec(lam.shape, lambda bi, h, i: (0, 0)),
                       pl.BlockSpec((1, hd), lambda bi, h, i: (0, 0))]
    else:
        extra = [kmean]
        extra_specs = [pl.BlockSpec((None, None, kmean.shape[2], hd), lambda bi, h, i: (bi, h, 0, 0))]
    return pl.pallas_call(
        functools.partial(_flash_prompt_kernel, diff=diff, lam_init=lam_init, tq=tq),
        grid=(b, nh, l // tq),
        in_specs=extra_specs + qkv_specs,
        out_specs=pl.BlockSpec((None, None, tq, hd), lambda bi, h, i: (bi, h, i, 0)),
        out_shape=jax.ShapeDtypeStruct((b, nh, l, hd), F32),
        scratch_shapes=[pltpu.VMEM((rows, hd), BF16), pltpu.VMEM((rows, 1), F32),
                        pltpu.VMEM((rows, 1), F32), pltpu.VMEM((rows, hd), F32)],
        compiler_params=_params("arbitrary", "arbitrary", "arbitrary"),
        name="flash_diff" if diff else "flash_moba",
    )(*extra, qkv, qkv, qkv)


def _block_mean_kernel(k_ref, o_ref):
    l, hd = k_ref.shape
    o_ref[...] = jnp.sum(k_ref[...].reshape(l // MOBA_BLOCK, MOBA_BLOCK, hd), axis=1) * (1.0 / MOBA_BLOCK)


def _block_mean_prompt(qkv):
    b, _, nh, l, hd = qkv.shape
    nb = l // MOBA_BLOCK
    return pl.pallas_call(
        _block_mean_kernel,
        grid=(b, nh),
        in_specs=[pl.BlockSpec((None, None, None, l, hd), lambda bi, h: (bi, 1, h, 0, 0))],
        out_specs=pl.BlockSpec((None, None, nb, hd), lambda bi, h: (bi, h, 0, 0)),
        out_shape=jax.ShapeDtypeStruct((b, nh, nb, hd), F32),
        compiler_params=_params("arbitrary", "arbitrary"),
        name="block_mean",
    )(qkv)


def _decode_kernel(*refs, mode, n_pg, paged, lam_init):
    refs = list(refs)
    if paged:
        refs.pop(0)
    q_ref = refs.pop(0)
    pg_refs = [refs.pop(0) for _ in range(n_pg)]
    kn_ref, vn_ref = refs.pop(0), refs.pop(0)
    bias_ref, bnew_ref = refs.pop(0), refs.pop(0)
    km_ref = refs.pop(0) if mode == "moba" else None
    if mode == "diff":
        lam_ref, gsub_ref = refs.pop(0), refs.pop(0)
    o_ref = refs.pop(0)
    lse_ref = refs.pop(0) if mode == "win" else None
    q_s, kf_s, vf_s, m_s, l_s, acc_s = refs[:6]
    sel_s = refs[6] if mode == "moba" else None

    s_id = pl.program_id(1)
    nh, n, hd = q_ref.shape
    rows = q_s.shape[0]
    half = rows // 2
    hpb = nh // 2
    tk = pg_refs[0].shape[0]
    scale = (DIFF_HALF if mode == "diff" else HEAD_DIM) ** -0.5

    @pl.when(s_id == 0)
    def _():
        lane = lax.broadcasted_iota(jnp.int32, (n, hd), 1)
        pieces = []
        for h in range(nh):
            qh = q_ref[h] * scale
            if mode == "diff":
                pieces += [jnp.where(lane < DIFF_HALF, qh, 0.0), jnp.where(lane >= DIFF_HALF, qh, 0.0)]
            else:
                pieces.append(qh)
        q_s[...] = jnp.concatenate(pieces, axis=0).astype(BF16)
        m_s[...] = jnp.full(m_s.shape, NEG, F32)
        l_s[...] = jnp.zeros(l_s.shape, F32)
        acc_s[...] = jnp.zeros(acc_s.shape, F32)
        if mode == "moba":
            n_blk = km_ref.shape[0] // nh
            for h in range(nh):
                gate = _dot_nt_3pass(q_ref[h], km_ref[pl.ds(h, n_blk, stride=nh), :])
                sel_s[h * n:(h + 1) * n, :] = _topk_bias(gate, n_blk, MOBA_TOPK)

    for hb in range(2):
        hs = slice(hb * hpb, (hb + 1) * hpb)
        rs = slice(hb * half, (hb + 1) * half)
        for r in range(n_pg):
            dst = slice(r * tk * hpb, (r + 1) * tk * hpb)
            kf_s[dst, :] = pg_refs[r][:, 0, hs, :].reshape(tk * hpb, hd).astype(BF16)
            vf_s[dst, :] = pg_refs[r][:, 1, hs, :].reshape(tk * hpb, hd).astype(BF16)
        s = _dot_nt(q_s[rs, :], kf_s[...]) + bias_ref[...]
        if mode == "moba":
            pages_per_blk = MOBA_BLOCK // tk
            cols = []
            for r in range(n_pg // pages_per_blk):
                col = _lane_column(sel_s[rs, :], s_id * (n_pg // pages_per_blk) + r)
                cols.append(jnp.broadcast_to(col, (half, MOBA_BLOCK * hpb)))
            s = s + jnp.concatenate(cols, axis=1)
        _online_update(s, vf_s[...], m_s.at[rs, :], l_s.at[rs, :], acc_s.at[rs, :])

    @pl.when(s_id == pl.num_programs(1) - 1)
    def _():
        pad = jnp.zeros((LANES - hpb * n, hd), F32)
        for hb in range(2):
            hs = slice(hb * hpb, (hb + 1) * hpb)
            rs = slice(hb * half, (hb + 1) * half)
            kn = jnp.concatenate([kn_ref[hs].reshape(hpb * n, hd), pad], axis=0).astype(BF16)
            vn = jnp.concatenate([vn_ref[hs].reshape(hpb * n, hd), pad], axis=0).astype(BF16)
            _online_update(_dot_nt(q_s[rs, :], kn) + bnew_ref[...], vn,
                           m_s.at[rs, :], l_s.at[rs, :], acc_s.at[rs, :])
        lsum = l_s[...]
        o = acc_s[...] * (1.0 / lsum)
        if mode == "diff":
            lam_full = _diff_lambda(lam_ref[...], lam_init)
            for h in range(nh):
                oh = o[2 * h * n:(2 * h + 1) * n] - lam_full * o[(2 * h + 1) * n:(2 * h + 2) * n]
                o_ref[h] = _sub_norm(oh, gsub_ref[...], lam_init)
        else:
            o_ref[...] = o.reshape(nh, n, hd)
        if mode == "win":
            lse_ref[...] = jnp.broadcast_to(m_s[...] + jnp.log(lsum), (rows, LANES)).reshape(nh, n, LANES)


def _head_match_bias(half_rows, rows_per_head, n_keys, hpb):
    r = np.arange(half_rows)[:, None] // rows_per_head
    c = np.arange(n_keys * hpb)[None, :] % hpb
    return np.where(r == c, 0.0, NEG)


def _decode_attn(qkv, gt, cache, layer, *, mode, page_table=None, dil=1, window=None,
                 kmean=None, lam=None, gsub=None, lam_init=0.0):
    b, _, nh, n, hd = qkv.shape
    hpb = nh // 2
    comps = 2 if mode == "diff" else 1
    rows = nh * comps * n
    half = rows // 2
    paged = page_table is not None
    tk = PAGE_SIZE
    total_pages = page_table.shape[1] if paged else cache.shape[2] // tk
    n_pg = min(PAGES_PER_STEP, total_pages)
    steps = total_pages // n_pg
    blk = (None, None, tk, 2, nh, hd)
    if paged:
        pg_specs = [pl.BlockSpec(blk, functools.partial(
            lambda bi, s, pt, r: (layer, pt[bi, s * n_pg + r], 0, 0, 0, 0), r=r)) for r in range(n_pg)]
    else:
        pg_specs = [pl.BlockSpec(blk, functools.partial(
            lambda bi, s, r: (layer, bi, s * n_pg + r, 0, 0, 0), r=r)) for r in range(n_pg)]

    def fixed(shape, idx):
        if paged:
            return pl.BlockSpec(shape, lambda bi, s, pt: idx(bi, s))
        return pl.BlockSpec(shape, lambda bi, s: idx(bi, s))

    qi = np.arange(half)[:, None] % n
    bias = _head_match_bias(half, comps * n, total_pages * tk if mode == "win" else n_pg * tk, hpb)
    cn = np.arange(LANES)[None, :]
    new_ok = (cn < hpb * n) & (cn // n == np.arange(half)[:, None] // (comps * n)) & (cn % n <= qi)
    if mode == "win":
        n_buf = total_pages * tk
        dist = n_buf + qi - np.arange(n_buf * hpb)[None, :] // hpb
        bias = np.where((dist % dil == 0) & (dist <= window), bias, NEG)
        new_ok &= (qi - cn % n) % dil == 0
    bias = jnp.asarray(bias, F32)
    bias_new = jnp.asarray(np.where(new_ok, 0.0, NEG), F32)

    head_blk = (None, None, nh, n, hd)
    args = [qkv] + [cache] * n_pg + [qkv, qkv, bias, bias_new]
    specs = [fixed(head_blk, lambda bi, s: (bi, gt, 0, 0, 0))] + pg_specs + [
        fixed(head_blk, lambda bi, s: (bi, gt + 1, 0, 0, 0)),
        fixed(head_blk, lambda bi, s: (bi, gt + 2, 0, 0, 0)),
        fixed((half, n_pg * tk * hpb), (lambda bi, s: (0, s)) if mode == "win" else (lambda bi, s: (0, 0))),
        fixed((half, LANES), lambda bi, s: (0, 0))]
    if mode == "moba":
        n_blk = kmean.shape[1]
        args.append(kmean.reshape(b, n_blk * nh, hd))
        specs.append(fixed((None, n_blk * nh, hd), lambda bi, s: (bi, 0, 0)))
    if mode == "diff":
        args += [lam, gsub.reshape(1, hd)]
        specs += [fixed(lam.shape, lambda bi, s: (0, 0)), fixed((1, hd), lambda bi, s: (0, 0))]
    scratch = [pltpu.VMEM((rows, hd), BF16),
               pltpu.VMEM((n_pg * tk * hpb, hd), BF16), pltpu.VMEM((n_pg * tk * hpb, hd), BF16),
               pltpu.VMEM((rows, 1), F32), pltpu.VMEM((rows, 1), F32), pltpu.VMEM((rows, hd), F32)]
    if mode == "moba":
        scratch.append(pltpu.VMEM((rows, n_blk), F32))
    out_blk = (None, nh, n, hd)
    out_shape = [jax.ShapeDtypeStruct((b, nh, n, hd), F32)]
    out_specs = [fixed(out_blk, lambda bi, s: (bi, 0, 0, 0))]
    if mode == "win":
        out_shape.append(jax.ShapeDtypeStruct((b, nh, n, LANES), F32))
        out_specs.append(fixed(out_blk, lambda bi, s: (bi, 0, 0, 0)))
    kern = functools.partial(_decode_kernel, mode=mode, n_pg=n_pg, paged=paged, lam_init=lam_init)
    name = "decode_" + mode
    if paged:
        grid_spec = pltpu.PrefetchScalarGridSpec(num_scalar_prefetch=1, grid=(b, steps), in_specs=specs,
                                                 out_specs=out_specs, scratch_shapes=scratch)
        out = pl.pallas_call(kern, grid_spec=grid_spec, out_shape=out_shape, name=name,
                             compiler_params=_params("arbitrary", "arbitrary"))(page_table, *args)
    else:
        out = pl.pallas_call(kern, grid=(b, steps), in_specs=specs, out_specs=out_specs, out_shape=out_shape,
                             scratch_shapes=scratch, name=name,
                             compiler_params=_params("arbitrary", "arbitrary"))(*args)
    return out if mode == "win" else out[0]


def _merge_groups_kernel(*refs):
    o_refs, lse_refs, out_ref = refs[:N_GROUPS], refs[N_GROUPS:2 * N_GROUPS], refs[2 * N_GROUPS]
    lses = [r[...] for r in lse_refs]
    m = jnp.maximum(jnp.maximum(lses[0], lses[1]), lses[2])
    es = [jnp.exp(v - m) for v in lses]
    inv = 1.0 / (es[0] + es[1] + es[2])
    out_ref[...] = sum(es[g] * o_refs[g][...] for g in range(N_GROUPS)) * inv


def _merge_groups(outs, lses):
    shape = outs[0].shape
    spec = pl.BlockSpec(shape, lambda i: (0,) * len(shape))
    return pl.pallas_call(
        _merge_groups_kernel, grid=(1,), in_specs=[spec] * (2 * N_GROUPS), out_specs=spec,
        out_shape=jax.ShapeDtypeStruct(shape, F32), compiler_params=_params("arbitrary"),
        name="merge_groups",
    )(*outs, *lses)


def _paged_mean_kernel(pt_ref, *refs, n_pages, pages_per_blk):
    o_ref = refs[n_pages]
    for j in range(n_pages // pages_per_blk):
        tot = sum(jnp.sum(refs[j * pages_per_blk + r][...], axis=0) for r in range(pages_per_blk))
        o_ref[j] = tot * (1.0 / MOBA_BLOCK)


def _paged_block_mean(cache, layer, page_table):
    b, total_pages = page_table.shape
    tk, _, nh, hd = cache.shape[2:]
    pages_per_blk = MOBA_BLOCK // tk
    n_pages = 8
    steps = total_pages // n_pages
    specs = [pl.BlockSpec((None, None, tk, None, nh, hd), functools.partial(
        lambda bi, s, pt, r: (layer, pt[bi, s * n_pages + r], 0, 0, 0, 0), r=r)) for r in range(n_pages)]
    return pl.pallas_call(
        functools.partial(_paged_mean_kernel, n_pages=n_pages, pages_per_blk=pages_per_blk),
        grid_spec=pltpu.PrefetchScalarGridSpec(
            num_scalar_prefetch=1, grid=(b, steps), in_specs=specs,
            out_specs=pl.BlockSpec((None, n_pages // pages_per_blk, nh, hd), lambda bi, s, pt: (bi, s, 0, 0))),
        out_shape=jax.ShapeDtypeStruct((b, total_pages // pages_per_blk, nh, hd), F32),
        compiler_params=_params("arbitrary", "arbitrary"),
        name="paged_block_mean",
    )(page_table, *([cache] * n_pages))


def _shift_windows_kernel(*refs):
    caches, news, outs = refs[:N_GROUPS], refs[N_GROUPS:2 * N_GROUPS], refs[2 * N_GROUPS:3 * N_GROUPS]
    sem = refs[3 * N_GROUPS]
    copies = []
    for g in range(N_GROUPS):
        layers, b, w = caches[g].shape[:3]
        n = news[g].shape[2]
        for j in range(layers):
            for bi in range(b):
                copies.append((caches[g].at[j, bi, pl.ds(n, w - n)], outs[g].at[j, bi, pl.ds(0, w - n)]))
        copies.append((news[g], outs[g].at[:, :, pl.ds(w - n, n)]))
    dmas = [pltpu.make_async_copy(src, dst, sem.at[i]) for i, (src, dst) in enumerate(copies)]
    for dma in dmas:
        dma.start()
    for dma in dmas:
        dma.wait()


def _shift_windows(caches, news):
    n_copies = sum(c.shape[0] * c.shape[1] + 1 for c in caches)
    hbm = pl.BlockSpec(memory_space=pl.ANY)
    return pl.pallas_call(
        _shift_windows_kernel,
        in_specs=[hbm] * (2 * N_GROUPS),
        out_specs=[hbm] * N_GROUPS,
        out_shape=[jax.ShapeDtypeStruct(c.shape, c.dtype) for c in caches],
        scratch_shapes=[pltpu.SemaphoreType.DMA((n_copies,))],
        name="shift_windows",
    )(*caches, *news)


def _rope_tables(pos, half, reps=1):
    inv = ROPE_THETA ** (-np.arange(0, 2 * half, 2, dtype=np.float64) / (2 * half))
    ang = np.asarray(pos, np.float64)[:, None] * inv[None, :]
    cos = np.concatenate([np.cos(ang), np.cos(ang)], axis=1)
    sin = np.concatenate([-np.sin(ang), np.sin(ang)], axis=1)
    tile = LANES // (2 * half)
    cos = np.tile(np.tile(cos, (1, tile)), (reps, 1))
    sin = np.tile(np.tile(sin, (1, tile)), (reps, 1))
    return (jnp.asarray(np.stack([cos, np.ones_like(cos)]), F32),
            jnp.asarray(np.stack([sin, np.zeros_like(sin)]), F32))


def _rows_of(kv_layers):
    return jnp.transpose(jnp.stack(kv_layers), (0, 1, 4, 2, 3, 5))


def kernel(x_prompt, x_sample, c_prompt, c_sample, cache_win1_kv, cache_win2_kv, cache_win3_kv, cache_kv,
           page_table, norm1_g, norm2_g, w_mod, b_mod, w_qkv_a, w_o_a, w_qkv_b, w_o_b, lambda_b, subln_g_b,
           w_qkv_c, w_o_c, w_up, w_down, final_g):
    bp, seq, d = x_prompt.shape
    bs, n_new, _ = x_sample.shape
    depth = w_mod.shape[0]
    past_len = page_table.shape[1] * PAGE_SIZE
    assert past_len % MOBA_BLOCK == 0 and seq % WIN_TILE == 0 and n_new == SUBLANES
    assert cache_kv.shape[2] == PAGE_SIZE
    pos_p = np.arange(seq)
    pos_s = past_len + np.arange(n_new)
    tl_p = 1024

    rope_p = {h: _rope_tables(pos_p, h) for h in (HEAD_DIM // 2, DIFF_HALF // 2)}
    rope_s = {h: _rope_tables(pos_s, h, reps=bs) for h in (HEAD_DIM // 2, DIFF_HALF // 2)}

    rows_c = 2 * SUBLANES
    c_all = jnp.concatenate([c_prompt, c_sample, jnp.zeros((rows_c - bp - bs, d), F32)], axis=0)
    mods = _modulation(c_all, w_mod, b_mod)
    win_caches = (cache_win1_kv, cache_win2_kv, cache_win3_kv)

    xp, xs = x_prompt, x_sample
    win_p = [[] for _ in range(N_GROUPS)]
    win_s = [[] for _ in range(N_GROUPS)]
    kv_p, kv_s = [], []
    for i in range(depth):
        kind = i % N_MIXERS
        j = i // N_MIXERS
        mp = [mods[i, :bp, t * d:(t + 1) * d][:, None, :] for t in range(6)]
        ms = [mods[i, bp:bp + bs, t * d:(t + 1) * d][:, None, :] for t in range(6)]
        if kind == 0:
            half = HEAD_DIM // 2
            qkv_p = _qkv(xp, norm1_g[i], mp[1], mp[0], *rope_p[half], w_qkv_a[j], half=half, tl=tl_p)
            qkv_s = _qkv(xs, norm1_g[i], ms[1], ms[0], *rope_s[half], w_qkv_a[j], half=half, tl=n_new)
            o_p = _win_prompt(qkv_p)
            outs_s, lses_s = [], []
            for g, (window, dil) in enumerate(DIL_GROUPS):
                keep = min(window, seq)
                win_p[g].append(qkv_p[:, 3 * g + 1:3 * g + 3, :, seq - keep:])
                o_g, lse_g = _decode_attn(qkv_s, 3 * g, win_caches[g], j, mode="win", dil=dil, window=window)
                outs_s.append(o_g)
                lses_s.append(lse_g)
                win_s[g].append(qkv_s[:, 3 * g + 1:3 * g + 3])
            o_s = _merge_groups(outs_s, lses_s)
            w_o = w_o_a[j]
        else:
            slot = i - (i + 2) // N_MIXERS
            if kind == 1:
                half = DIFF_HALF // 2
                w_qkv, w_o = w_qkv_b[j], w_o_b[j]
            else:
                half = HEAD_DIM // 2
                w_qkv, w_o = w_qkv_c[j], w_o_c[j]
            qkv_p = _qkv(xp, norm1_g[i], mp[1], mp[0], *rope_p[half], w_qkv, half=half, tl=tl_p)
            qkv_s = _qkv(xs, norm1_g[i], ms[1], ms[0], *rope_s[half], w_qkv, half=half, tl=n_new)
            if kind == 1:
                lam_init = 0.8 - 0.6 * math.exp(-0.3 * i)
                o_p = _flash_prompt(qkv_p, diff=True, lam=lambda_b[j], gsub=subln_g_b[j], lam_init=lam_init)
                o_s = _decode_attn(qkv_s, 0, cache_kv, slot, mode="diff", page_table=page_table,
                                   lam=lambda_b[j], gsub=subln_g_b[j], lam_init=lam_init)
            else:
                o_p = _flash_prompt(qkv_p, diff=False, kmean=_block_mean_prompt(qkv_p))
                o_s = _decode_attn(qkv_s, 0, cache_kv, slot, mode="moba", page_table=page_table,
                                   kmean=_paged_block_mean(cache_kv, slot, page_table))
            kv_p.append(qkv_p[:, 1:3])
            kv_s.append(qkv_s[:, 1:3])
        xp = _proj(o_p, w_o, mp[2], xp, tl=tl_p)
        xs = _proj(o_s, w_o, ms[2], xs, tl=n_new)
        xp = _mlp(xp, norm2_g[i], mp[4], mp[3], mp[5], w_up[i], w_down[i], tl=tl_p)
        xs = _mlp(xs, norm2_g[i], ms[4], ms[3], ms[5], w_up[i], w_down[i], tl=n_new)
    y_prompt = _final_norm(xp, final_g)
    y_sample = _final_norm(xs, final_g)
    new_win_s = _shift_windows(win_caches, [_rows_of(win_s[g]) for g in range(N_GROUPS)])
    return (y_prompt, y_sample,
            _rows_of(win_p[0]), _rows_of(win_p[1]), _rows_of(win_p[2]), _rows_of(kv_p),
            new_win_s[0], new_win_s[1], new_win_s[2], _rows_of(kv_s))
```

```python
import functools
import math

import numpy as np
import jax
import jax.numpy as jnp
from jax import lax
from jax.experimental import pallas as pl
from jax.experimental.pallas import tpu as pltpu

F32 = jnp.float32
BF16 = jnp.bfloat16

HEAD_DIM = 128
DIL_GROUPS = ((128, 1), (512, 4), (2048, 16))
N_GROUPS = len(DIL_GROUPS)
DIFF_HALF = HEAD_DIM // 2
MOBA_BLOCK = 256
MOBA_TOPK = 3
PAGE_SIZE = 128
ROPE_THETA = 10000.0
NORM_EPS = 1e-6
SUBLN_EPS = 1e-5
N_MIXERS = 3

LANES = 128
SUBLANES = 8
NEG = -1e30
VMEM_LIMIT = 56 * 1024 * 1024
PAGES_PER_STEP = 4
ROW_CHUNK = 128
WIN_TILE = 2048
WIN_UNROLL = 4
FLASH_CHUNK = 1024
FLASH_ROWS = 128


def _params(*sem):
    return pltpu.CompilerParams(dimension_semantics=sem, vmem_limit_bytes=VMEM_LIMIT)


def _dot(a, b):
    return jnp.dot(a, b, preferred_element_type=F32)


def _dot_nt(a, b):
    return lax.dot_general(a, b, (((1,), (1,)), ((), ())), preferred_element_type=F32)


def _split_bf16(x):
    hi = x.astype(BF16)
    lo = (x - hi.astype(F32)).astype(BF16)
    return hi, lo


def _dot_nt_3pass(a, b):
    ah, al = _split_bf16(a)
    bh, bl = _split_bf16(b)
    return _dot_nt(ah, bh) + _dot_nt(ah, bl) + _dot_nt(al, bh)


def _norm_mod(x, g, scale, shift):
    y = x * lax.rsqrt(jnp.mean(x * x, axis=-1, keepdims=True) + NORM_EPS) * g
    return y * (1.0 + scale) + shift


def _for_row_chunks(bb, tl, body):
    if bb > 1 or tl <= ROW_CHUNK:
        body(slice(None))
        return

    def step(c, carry):
        body(pl.ds(pl.multiple_of(c * ROW_CHUNK, ROW_CHUNK), ROW_CHUNK))
        return carry

    lax.fori_loop(0, tl // ROW_CHUNK, step, 0)


def _store_norm_mod(x_ref, g_ref, sc_ref, sh_ref, h_s):
    bb, tl, d = x_ref.shape

    def body(rows):
        h = _norm_mod(x_ref[:, rows, :], g_ref[...], sc_ref[...], sh_ref[...])
        h_s[rows, :] = h.reshape(-1, d).astype(BF16)

    _for_row_chunks(bb, tl, body)


def _batch_block(b, l, tl):
    return b if tl == l and b * l <= 1024 else 1


def _mod_kernel(c_ref, w_ref, b_ref, o_ref):
    c = c_ref[...]
    a = (c / (1.0 + jnp.exp(-c))).astype(BF16)
    o_ref[...] = _dot(a, w_ref[...].astype(BF16)) + b_ref[...]


def _modulation(c_all, w_mod, b_mod):
    depth, d, n = w_mod.shape
    rows = c_all.shape[0]
    tn = 1024
    return pl.pallas_call(
        _mod_kernel,
        grid=(depth, n // tn),
        in_specs=[pl.BlockSpec((rows, d), lambda i, j: (0, 0)),
                  pl.BlockSpec((None, d, tn), lambda i, j: (i, 0, j)),
                  pl.BlockSpec((None, 1, tn), lambda i, j: (i, 0, j))],
        out_specs=pl.BlockSpec((None, rows, tn), lambda i, j: (i, 0, j)),
        out_shape=jax.ShapeDtypeStruct((depth, rows, n), F32),
        compiler_params=_params("arbitrary", "arbitrary"),
        name="modulation",
    )(c_all, w_mod, b_mod.reshape(depth, 1, n))


def _rope_chunk(x, cos, sin, half):
    if 2 * half == LANES:
        rot = pltpu.roll(x, half, 1)
    else:
        lane = lax.broadcasted_iota(jnp.int32, x.shape, 1)
        rot = jnp.where(lane % (2 * half) < half, pltpu.roll(x, LANES - half, 1), pltpu.roll(x, half, 1))
    return x * cos + rot * sin


def _qkv_kernel(x_ref, g_ref, sc_ref, sh_ref, cos_ref, sin_ref, w_ref, o_ref, h_s, *, half):
    n = pl.program_id(2)
    bb, tl, d = x_ref.shape
    tn = w_ref.shape[1]

    @pl.when(n == 0)
    def _():
        _store_norm_mod(x_ref, g_ref, sc_ref, sh_ref, h_s)

    acc = _dot(h_s[...], w_ref[...].astype(BF16))
    cos = cos_ref[...]
    sin = sin_ref[...]
    for c in range(tn // LANES):
        chunk = _rope_chunk(acc[:, c * LANES:(c + 1) * LANES], cos, sin, half)
        o_ref[:, c, :, :] = chunk.reshape(bb, tl, LANES)


def _qkv(x, g, scale, shift, cos, sin, w, *, half, tl):
    b, l, d = x.shape
    n = w.shape[1]
    nh = d // HEAD_DIM
    tn = 512
    hpt = tn // HEAD_DIM
    tpk = d // tn
    bb = _batch_block(b, l, tl)
    table = pl.BlockSpec((None, bb * tl, LANES), lambda i, j, k: (((k // tpk) % 3) // 2, j, 0))
    return pl.pallas_call(
        functools.partial(_qkv_kernel, half=half),
        grid=(b // bb, l // tl, n // tn),
        in_specs=[pl.BlockSpec((bb, tl, d), lambda i, j, k: (i, j, 0)),
                  pl.BlockSpec((1, d), lambda i, j, k: (0, 0)),
                  pl.BlockSpec((bb, 1, d), lambda i, j, k: (i, 0, 0)),
                  pl.BlockSpec((bb, 1, d), lambda i, j, k: (i, 0, 0)),
                  table, table,
                  pl.BlockSpec((d, tn), lambda i, j, k: (0, k))],
        out_specs=pl.BlockSpec((bb, None, hpt, tl, HEAD_DIM), lambda i, j, k: (i, k // tpk, k % tpk, j, 0)),
        out_shape=jax.ShapeDtypeStruct((b, n // d, nh, l, HEAD_DIM), F32),
        scratch_shapes=[pltpu.VMEM((bb * tl, d), BF16)],
        compiler_params=_params("arbitrary", "arbitrary", "arbitrary"),
        name="qkv",
    )(x, g.reshape(1, d), scale, shift, cos, sin, w)


def _proj_kernel(a_ref, w_ref, gate_ref, res_ref, o_ref, a_s):
    n = pl.program_id(2)
    bb, nh, tl, hd = a_ref.shape
    tn = w_ref.shape[1]

    @pl.when(n == 0)
    def _():
        def body(rows):
            for h in range(nh):
                a_s[rows, h * hd:(h + 1) * hd] = a_ref[:, h, rows, :].reshape(-1, hd).astype(BF16)

        _for_row_chunks(bb, tl, body)

    acc = _dot(a_s[...], w_ref[...].astype(BF16))
    o_ref[...] = res_ref[...] + gate_ref[...] * acc.reshape(bb, tl, tn)


def _proj(a, w, gate, res, *, tl):
    b, l, d = res.shape
    nh = a.shape[1]
    tn = 512
    bb = _batch_block(b, l, tl)
    return pl.pallas_call(
        _proj_kernel,
        grid=(b // bb, l // tl, d // tn),
        in_specs=[pl.BlockSpec((bb, nh, tl, HEAD_DIM), lambda i, j, k: (i, 0, j, 0)),
                  pl.BlockSpec((d, tn), lambda i, j, k: (0, k)),
                  pl.BlockSpec((bb, 1, tn), lambda i, j, k: (i, 0, k)),
                  pl.BlockSpec((bb, tl, tn), lambda i, j, k: (i, j, k))],
        out_specs=pl.BlockSpec((bb, tl, tn), lambda i, j, k: (i, j, k)),
        out_shape=jax.ShapeDtypeStruct((b, l, d), F32),
        scratch_shapes=[pltpu.VMEM((bb * tl, d), BF16)],
        compiler_params=_params("arbitrary", "arbitrary", "arbitrary"),
        name="proj",
    )(a, w, gate, res)


def _mlp_kernel(x_ref, g_ref, sc_ref, sh_ref, gate_ref, wu_ref, wd_ref, o_ref, h_s):
    f = pl.program_id(2)
    bb, tl, d = x_ref.shape
    nchunk = 512

    @pl.when(f == 0)
    def _():
        _store_norm_mod(x_ref, g_ref, sc_ref, sh_ref, h_s)
        o_ref[...] = jnp.zeros_like(o_ref)

    a = jnp.maximum(_dot(h_s[...], wu_ref[...].astype(BF16)), 0.0)
    a = (a * a).astype(BF16)
    for c in range(d // nchunk):
        sl = slice(c * nchunk, (c + 1) * nchunk)
        o_ref[:, :, sl] += _dot(a, wd_ref[:, sl].astype(BF16)).reshape(bb, tl, nchunk)

    @pl.when(f == pl.num_programs(2) - 1)
    def _():
        o_ref[...] = x_ref[...] + gate_ref[...] * o_ref[...]


def _mlp(x, g, scale, shift, gate, w_up, w_down, *, tl):
    b, l, d = x.shape
    ff = w_up.shape[1]
    bb = _batch_block(b, l, tl)
    big = bb * tl >= 1024
    tf = 512
    vec = pl.BlockSpec((bb, 1, d), lambda i, j, k: (i, 0, 0))
    x_mode = dict(pipeline_mode=pl.Buffered(1)) if big else {}
    return pl.pallas_call(
        _mlp_kernel,
        grid=(b // bb, l // tl, ff // tf),
        in_specs=[pl.BlockSpec((bb, tl, d), lambda i, j, k: (i, j, 0), **x_mode),
                  pl.BlockSpec((1, d), lambda i, j, k: (0, 0)),
                  vec, vec, vec,
                  pl.BlockSpec((d, tf), lambda i, j, k: (0, k)),
                  pl.BlockSpec((tf, d), lambda i, j, k: (k, 0))],
        out_specs=pl.BlockSpec((bb, tl, d), lambda i, j, k: (i, j, 0), **x_mode),
        out_shape=jax.ShapeDtypeStruct((b, l, d), F32),
        scratch_shapes=[pltpu.VMEM((bb * tl, d), BF16)],
        compiler_params=_params("arbitrary", "arbitrary", "arbitrary"),
        name="mlp",
    )(x, g.reshape(1, d), scale, shift, gate, w_up, w_down)


def _final_norm_kernel(x_ref, g_ref, o_ref):
    x = x_ref[...]
    o_ref[...] = x * lax.rsqrt(jnp.mean(x * x, axis=-1, keepdims=True) + NORM_EPS) * g_ref[...]


def _final_norm(x, g):
    b, l, d = x.shape
    tl = min(l, 512)
    return pl.pallas_call(
        _final_norm_kernel,
        grid=(b, l // tl),
        in_specs=[pl.BlockSpec((1, tl, d), lambda i, j: (i, j, 0)),
                  pl.BlockSpec((1, d), lambda i, j: (0, 0))],
        out_specs=pl.BlockSpec((1, tl, d), lambda i, j: (i, j, 0)),
        out_shape=jax.ShapeDtypeStruct((b, l, d), F32),
        compiler_params=_params("arbitrary", "arbitrary"),
        name="final_norm",
    )(x, g.reshape(1, d))


def _online_update(s, v, m_s, l_s, acc_s):
    m_prev = m_s[...]
    m_new = jnp.maximum(m_prev, jnp.max(s, axis=1, keepdims=True))
    alpha = jnp.exp(m_prev - m_new)
    p = jnp.exp(s - m_new)
    l_s[...] = alpha * l_s[...] + jnp.sum(p, axis=1, keepdims=True)
    acc_s[...] = alpha * acc_s[...] + _dot(p.astype(BF16), v)
    m_s[...] = m_new


def _lane_column(x, j):
    lane = lax.broadcasted_iota(jnp.int32, x.shape, 1)
    return jnp.sum(jnp.where(lane == j, x, 0.0), axis=1, keepdims=True)


def _topk_bias(gate, n_valid, k):
    lane = lax.broadcasted_iota(jnp.int32, gate.shape, 1).astype(F32)
    g = jnp.where(lane < n_valid, gate, NEG)
    bias = jnp.full(gate.shape, NEG, F32)
    for _ in range(k):
        mx = jnp.max(g, axis=1, keepdims=True)
        first = jnp.min(jnp.where(g == mx, lane, float(gate.shape[1])), axis=1, keepdims=True)
        pick = (lane == first) & (mx > 0.5 * NEG)
        bias = jnp.where(pick, 0.0, bias)
        g = jnp.where(lane == first, NEG, g)
    return bias


def _diff_lambda(lam, lam_init):
    return (jnp.exp(jnp.sum(lam[0:1] * lam[1:2], axis=1, keepdims=True))
            - jnp.exp(jnp.sum(lam[2:3] * lam[3:4], axis=1, keepdims=True)) + lam_init)


def _sub_norm(o, gsub, lam_init):
    return o * lax.rsqrt(jnp.mean(o * o, axis=-1, keepdims=True) + SUBLN_EPS) * gsub * (1.0 - lam_init)


def _win_prompt_kernel(*refs, nw):
    in_refs, (o_ref, kcat_s, vcat_s, og_s, lse_s) = refs[:5 * N_GROUPS], refs[5 * N_GROUPS:]
    t = pl.program_id(2)
    tile = o_ref.shape[0]
    scale = HEAD_DIM ** -0.5
    r = lax.broadcasted_iota(jnp.int32, (nw, 2 * nw), 0)
    c = lax.broadcasted_iota(jnp.int32, (nw, 2 * nw), 1)
    dist = nw + r - c
    band = (dist >= 0) & (dist <= nw)

    for g, (window, dil) in enumerate(DIL_GROUPS):
        q_ref, kc_ref, kp_ref, vc_ref, vp_ref = in_refs[5 * g:5 * g + 5]
        kcat_s[0:window, :] = kp_ref[...]
        kcat_s[window:window + tile, :] = kc_ref[...]
        vcat_s[0:window, :] = vp_ref[...]
        vcat_s[window:window + tile, :] = vc_ref[...]

        def blocks(ug, carry, g=g, dil=dil, window=window, q_ref=q_ref):
            work = []
            for i in range(WIN_UNROLL):
                u = ug * WIN_UNROLL + i
                start = u % dil + (u // dil) * window
                if dil == 1:
                    rows_q = pl.ds(start, nw)
                    rows_k = pl.ds(start, 2 * nw)
                else:
                    rows_q = pl.ds(start, nw, stride=dil)
                    rows_k = pl.ds(start, 2 * nw, stride=dil)
                q = (q_ref[rows_q, :] * scale).astype(BF16)
                k = kcat_s[rows_k, :].astype(BF16)
                first_key = jnp.where((t > 0) | (u >= dil), 0, nw)
                s = _dot_nt(q, k) + jnp.where(band & (c >= first_key), 0.0, NEG)
                work.append((rows_q, rows_k, s))
            for rows_q, rows_k, s in work:
                v = vcat_s[rows_k, :].astype(BF16)
                m = jnp.max(s, axis=1, keepdims=True)
                p = jnp.exp(s - m)
                l = jnp.sum(p, axis=1, keepdims=True)
                og_s[g, rows_q, :] = _dot(p.astype(BF16), v) * (1.0 / l)
                lse_s[g, rows_q, :] = jnp.broadcast_to(m + jnp.log(l), (nw, LANES))
            return carry

        lax.fori_loop(0, tile // nw // WIN_UNROLL, blocks, 0)

    def merge(rows):
        lses = [lse_s[g, rows, :] for g in range(N_GROUPS)]
        m = jnp.maximum(jnp.maximum(lses[0], lses[1]), lses[2])
        es = [jnp.exp(v - m) for v in lses]
        inv = 1.0 / (es[0] + es[1] + es[2])
        o_ref[rows, :] = sum(es[g] * og_s[g, rows, :] for g in range(N_GROUPS)) * inv

    _for_row_chunks(1, tile, merge)


def _win_prompt(qkv):
    b, _, nh, l, hd = qkv.shape
    tile = WIN_TILE
    nw = DIL_GROUPS[0][0] // DIL_GROUPS[0][1]
    specs = []
    for g, (window, dil) in enumerate(DIL_GROUPS):
        assert window // dil == nw and tile % window == 0

        def cur(t, g=g):
            return pl.BlockSpec((None, None, None, tile, hd), lambda bi, h, i: (bi, 3 * g + t, h, i, 0))

        def prev(t, g=g, per=tile // window, window=window):
            return pl.BlockSpec((None, None, None, window, hd),
                                lambda bi, h, i: (bi, 3 * g + t, h, jnp.maximum(i * per - 1, 0), 0))

        specs += [cur(0), cur(1), prev(1), cur(2), prev(2)]
    max_w = DIL_GROUPS[-1][0]
    return pl.pallas_call(
        functools.partial(_win_prompt_kernel, nw=nw),
        grid=(b, nh, l // tile),
        in_specs=specs,
        out_specs=pl.BlockSpec((None, None, tile, hd), lambda bi, h, i: (bi, h, i, 0)),
        out_shape=jax.ShapeDtypeStruct((b, nh, l, hd), F32),
        scratch_shapes=[pltpu.VMEM((max_w + tile, hd), F32), pltpu.VMEM((max_w + tile, hd), F32),
                        pltpu.VMEM((N_GROUPS, tile, hd), F32), pltpu.VMEM((N_GROUPS, tile, LANES), F32)],
        compiler_params=_params("arbitrary", "arbitrary", "arbitrary"),
        name="win_prompt",
    )(*([qkv] * (5 * N_GROUPS)))


def _flash_prompt_kernel(*refs, diff, lam_init, tq):
    if diff:
        lam_ref, gsub_ref, q_ref, k_ref, v_ref, o_ref, qs_s, m_s, l_s, acc_s = refs
    else:
        km_ref, q_ref, k_ref, v_ref, o_ref, qs_s, m_s, l_s, acc_s = refs
    qi = pl.program_id(2)
    rows = qs_s.shape[0]
    lane = lax.broadcasted_iota(jnp.int32, (tq, HEAD_DIM), 1)
    q = q_ref[...]
    if diff:
        qsc = q * (DIFF_HALF ** -0.5)
        qs_s[...] = jnp.concatenate([jnp.where(lane < DIFF_HALF, qsc, 0.0),
                                     jnp.where(lane >= DIFF_HALF, qsc, 0.0)], axis=0).astype(BF16)
    else:
        qs_s[...] = (q * (HEAD_DIM ** -0.5)).astype(BF16)
        own = (qi * tq + lax.broadcasted_iota(jnp.int32, (tq, 1), 0)) // MOBA_BLOCK
        sel_bias = _topk_bias(_dot_nt_3pass(q, km_ref[...]), own.astype(F32), MOBA_TOPK)
    m_s[...] = jnp.full(m_s.shape, NEG, F32)
    l_s[...] = jnp.zeros(l_s.shape, F32)
    acc_s[...] = jnp.zeros(acc_s.shape, F32)

    chunk = FLASH_CHUNK
    diag = qi // (chunk // tq)

    def selection_bias(ci):
        per_chunk = chunk // MOBA_BLOCK
        cols = [jnp.broadcast_to(_lane_column(sel_bias, ci * per_chunk + t), (tq, MOBA_BLOCK))
                for t in range(per_chunk)]
        return jnp.concatenate(cols, axis=1)

    def consume(ci, bias):
        rows_k = pl.ds(pl.multiple_of(ci * chunk, chunk), chunk)
        k = k_ref[rows_k, :].astype(BF16)
        v = v_ref[rows_k, :].astype(BF16)
        sub = FLASH_ROWS
        scores = [_dot_nt(qs_s[r0:r0 + sub, :], k) for r0 in range(0, rows, sub)]
        for r0, s in zip(range(0, rows, sub), scores):
            rs = slice(r0, r0 + sub)
            if bias is not None:
                s = s + bias[r0 % tq:r0 % tq + sub]
            _online_update(s, v, m_s.at[rs, :], l_s.at[rs, :], acc_s.at[rs, :])

    def past_chunk(ci, carry):
        consume(ci, None if diff else selection_bias(ci))
        return carry

    lax.fori_loop(0, diag, past_chunk, 0)

    q_pos = qi * tq + lax.broadcasted_iota(jnp.int32, (tq, chunk), 0)
    k_pos = diag * chunk + lax.broadcasted_iota(jnp.int32, (tq, chunk), 1)
    bias = jnp.where(k_pos <= q_pos, 0.0, NEG)
    if not diff:
        bias = jnp.where(k_pos // MOBA_BLOCK >= q_pos // MOBA_BLOCK, bias, selection_bias(diag))
    consume(diag, bias)

    o = acc_s[...] * (1.0 / l_s[...])
    if diff:
        o = o[:tq] - _diff_lambda(lam_ref[...], lam_init) * o[tq:]
        o = _sub_norm(o, gsub_ref[...], lam_init)
    o_ref[...] = o


def _flash_prompt(qkv, *, diff, lam=None, gsub=None, lam_init=0.0, kmean=None):
    b, _, nh, l, hd = qkv.shape
    tq = MOBA_BLOCK if diff else 2 * MOBA_BLOCK
    assert l % FLASH_CHUNK == 0 and FLASH_CHUNK % tq == 0
    rows = 2 * tq if diff else tq
    qkv_specs = [pl.BlockSpec((None, None, None, tq, hd), lambda bi, h, i: (bi, 0, h, i, 0)),
                 pl.BlockSpec((None, None, None, l, hd), lambda bi, h, i: (bi, 1, h, 0, 0)),
                 pl.BlockSpec((None, None, None, l, hd), lambda bi, h, i: (bi, 2, h, 0, 0))]
    if diff:
        extra = [lam, gsub.reshape(1, hd)]
        extra_specs = [pl.BlockSpec(lam.shape, lambda bi, h, i: (0, 0)),
                       pl.BlockSpec((1, hd), lambda bi, h, i: (0, 0))]
    else:
        extra = [kmean]
        extra_specs = [pl.BlockSpec((None, None, kmean.shape[2], hd), lambda bi, h, i: (bi, h, 0, 0))]
    return pl.pallas_call(
        functools.partial(_flash_prompt_kernel, diff=diff, lam_init=lam_init, tq=tq),
        grid=(b, nh, l // tq),
        in_specs=extra_specs + qkv_specs,
        out_specs=pl.BlockSpec((None, None, tq, hd), lambda bi, h, i: (bi, h, i, 0)),
        out_shape=jax.ShapeDtypeStruct((b, nh, l, hd), F32),
        scratch_shapes=[pltpu.VMEM((rows, hd), BF16), pltpu.VMEM((rows, 1), F32),
                        pltpu.VMEM((rows, 1), F32), pltpu.VMEM((rows, hd), F32)],
        compiler_params=_params("arbitrary", "arbitrary", "arbitrary"),
        name="flash_diff" if diff else "flash_moba",
    )(*extra, qkv, qkv, qkv)


def _block_mean_kernel(k_ref, o_ref):
    l, hd = k_ref.shape
    o_ref[...] = jnp.sum(k_ref[...].reshape(l // MOBA_BLOCK, MOBA_BLOCK, hd), axis=1) * (1.0 / MOBA_BLOCK)


def _block_mean_prompt(qkv):
    b, _, nh, l, hd = qkv.shape
    nb = l // MOBA_BLOCK
    return pl.pallas_call(
        _block_mean_kernel,
        grid=(b, nh),
        in_specs=[pl.BlockSpec((None, None, None, l, hd), lambda bi, h: (bi, 1, h, 0, 0))],
        out_specs=pl.BlockSpec((None, None, nb, hd), lambda bi, h: (bi, h, 0, 0)),
        out_shape=jax.ShapeDtypeStruct((b, nh, nb, hd), F32),
        compiler_params=_params("arbitrary", "arbitrary"),
        name="block_mean",
    )(qkv)


def _decode_kernel(*refs, mode, n_pg, paged, lam_init):
    refs = list(refs)
    if paged:
        refs.pop(0)
    q_ref = refs.pop(0)
    pg_refs = [refs.pop(0) for _ in range(n_pg)]
    kn_ref, vn_ref = refs.pop(0), refs.pop(0)
    bias_ref, bnew_ref = refs.pop(0), refs.pop(0)
    km_ref = refs.pop(0) if mode == "moba" else None
    if mode == "diff":
        lam_ref, gsub_ref = refs.pop(0), refs.pop(0)
    o_ref = refs.pop(0)
    lse_ref = refs.pop(0) if mode == "win" else None
    q_s, kf_s, vf_s, m_s, l_s, acc_s = refs[:6]
    sel_s = refs[6] if mode == "moba" else None

    s_id = pl.program_id(1)
    nh, n, hd = q_ref.shape
    rows = q_s.shape[0]
    half = rows // 2
    hpb = nh // 2
    tk = pg_refs[0].shape[0]
    scale = (DIFF_HALF if mode == "diff" else HEAD_DIM) ** -0.5

    @pl.when(s_id == 0)
    def _():
        lane = lax.broadcasted_iota(jnp.int32, (n, hd), 1)
        pieces = []
        for h in range(nh):
            qh = q_ref[h] * scale
            if mode == "diff":
                pieces += [jnp.where(lane < DIFF_HALF, qh, 0.0), jnp.where(lane >= DIFF_HALF, qh, 0.0)]
            else:
                pieces.append(qh)
        q_s[...] = jnp.concatenate(pieces, axis=0).astype(BF16)
        m_s[...] = jnp.full(m_s.shape, NEG, F32)
        l_s[...] = jnp.zeros(l_s.shape, F32)
        acc_s[...] = jnp.zeros(acc_s.shape, F32)
        if mode == "moba":
            n_blk = km_ref.shape[0] // nh
            for h in range(nh):
                gate = _dot_nt_3pass(q_ref[h], km_ref[pl.ds(h, n_blk, stride=nh), :])
                sel_s[h * n:(h + 1) * n, :] = _topk_bias(gate, n_blk, MOBA_TOPK)

    for hb in range(2):
        hs = slice(hb * hpb, (hb + 1) * hpb)
        rs = slice(hb * half, (hb + 1) * half)
        for r in range(n_pg):
            dst = slice(r * tk * hpb, (r + 1) * tk * hpb)
            kf_s[dst, :] = pg_refs[r][:, 0, hs, :].reshape(tk * hpb, hd).astype(BF16)
            vf_s[dst, :] = pg_refs[r][:, 1, hs, :].reshape(tk * hpb, hd).astype(BF16)
        s = _dot_nt(q_s[rs, :], kf_s[...]) + bias_ref[...]
        if mode == "moba":
            pages_per_blk = MOBA_BLOCK // tk
            cols = []
            for r in range(n_pg // pages_per_blk):
                col = _lane_column(sel_s[rs, :], s_id * (n_pg // pages_per_blk) + r)
                cols.append(jnp.broadcast_to(col, (half, MOBA_BLOCK * hpb)))
            s = s + jnp.concatenate(cols, axis=1)
        _online_update(s, vf_s[...], m_s.at[rs, :], l_s.at[rs, :], acc_s.at[rs, :])

    @pl.when(s_id == pl.num_programs(1) - 1)
    def _():
        pad = jnp.zeros((LANES - hpb * n, hd), F32)
        for hb in range(2):
            hs = slice(hb * hpb, (hb + 1) * hpb)
            rs = slice(hb * half, (hb + 1) * half)
            kn = jnp.concatenate([kn_ref[hs].reshape(hpb * n, hd), pad], axis=0).astype(BF16)
            vn = jnp.concatenate([vn_ref[hs].reshape(hpb * n, hd), pad], axis=0).astype(BF16)
            _online_update(_dot_nt(q_s[rs, :], kn) + bnew_ref[...], vn,
                           m_s.at[rs, :], l_s.at[rs, :], acc_s.at[rs, :])
        lsum = l_s[...]
        o = acc_s[...] * (1.0 / lsum)
        if mode == "diff":
            lam_full = _diff_lambda(lam_ref[...], lam_init)
            for h in range(nh):
                oh = o[2 * h * n:(2 * h + 1) * n] - lam_full * o[(2 * h + 1) * n:(2 * h + 2) * n]
                o_ref[h] = _sub_norm(oh, gsub_ref[...], lam_init)
        else:
            o_ref[...] = o.reshape(nh, n, hd)
        if mode == "win":
            lse_ref[...] = jnp.broadcast_to(m_s[...] + jnp.log(lsum), (rows, LANES)).reshape(nh, n, LANES)


def _head_match_bias(half_rows, rows_per_head, n_keys, hpb):
    r = np.arange(half_rows)[:, None] // rows_per_head
    c = np.arange(n_keys * hpb)[None, :] % hpb
    return np.where(r == c, 0.0, NEG)


def _decode_attn(qkv, gt, cache, layer, *, mode, page_table=None, dil=1, window=None,
                 kmean=None, lam=None, gsub=None, lam_init=0.0):
    b, _, nh, n, hd = qkv.shape
    hpb = nh // 2
    comps = 2 if mode == "diff" else 1
    rows = nh * comps * n
    half = rows // 2
    paged = page_table is not None
    tk = PAGE_SIZE
    total_pages = page_table.shape[1] if paged else cache.shape[2] // tk
    n_pg = min(PAGES_PER_STEP, total_pages)
    steps = total_pages // n_pg
    blk = (None, None, tk, 2, nh, hd)
    if paged:
        pg_specs = [pl.BlockSpec(blk, functools.partial(
            lambda bi, s, pt, r: (layer, pt[bi, s * n_pg + r], 0, 0, 0, 0), r=r)) for r in range(n_pg)]
    else:
        pg_specs = [pl.BlockSpec(blk, functools.partial(
            lambda bi, s, r: (layer, bi, s * n_pg + r, 0, 0, 0), r=r)) for r in range(n_pg)]

    def fixed(shape, idx):
        if paged:
            return pl.BlockSpec(shape, lambda bi, s, pt: idx(bi, s))
        return pl.BlockSpec(shape, lambda bi, s: idx(bi, s))

    qi = np.arange(half)[:, None] % n
    bias = _head_match_bias(half, comps * n, total_pages * tk if mode == "win" else n_pg * tk, hpb)
    cn = np.arange(LANES)[None, :]
    new_ok = (cn < hpb * n) & (cn // n == np.arange(half)[:, None] // (comps * n)) & (cn % n <= qi)
    if mode == "win":
        n_buf = total_pages * tk
        dist = n_buf + qi - np.arange(n_buf * hpb)[None, :] // hpb
        bias = np.where((dist % dil == 0) & (dist <= window), bias, NEG)
        new_ok &= (qi - cn % n) % dil == 0
    bias = jnp.asarray(bias, F32)
    bias_new = jnp.asarray(np.where(new_ok, 0.0, NEG), F32)

    head_blk = (None, None, nh, n, hd)
    args = [qkv] + [cache] * n_pg + [qkv, qkv, bias, bias_new]
    specs = [fixed(head_blk, lambda bi, s: (bi, gt, 0, 0, 0))] + pg_specs + [
        fixed(head_blk, lambda bi, s: (bi, gt + 1, 0, 0, 0)),
        fixed(head_blk, lambda bi, s: (bi, gt + 2, 0, 0, 0)),
        fixed((half, n_pg * tk * hpb), (lambda bi, s: (0, s)) if mode == "win" else (lambda bi, s: (0, 0))),
        fixed((half, LANES), lambda bi, s: (0, 0))]
    if mode == "moba":
        n_blk = kmean.shape[1]
        args.append(kmean.reshape(b, n_blk * nh, hd))
        specs.append(fixed((None, n_blk * nh, hd), lambda bi, s: (bi, 0, 0)))
    if mode == "diff":
        args += [lam, gsub.reshape(1, hd)]
        specs += [fixed(lam.shape, lambda bi, s: (0, 0)), fixed((1, hd), lambda bi, s: (0, 0))]
    scratch = [pltpu.VMEM((rows, hd), BF16),
               pltpu.VMEM((n_pg * tk * hpb, hd), BF16), pltpu.VMEM((n_pg * tk * hpb, hd), BF16),
               pltpu.VMEM((rows, 1), F32), pltpu.VMEM((rows, 1), F32), pltpu.VMEM((rows, hd), F32)]
    if mode == "moba":
        scratch.append(pltpu.VMEM((rows, n_blk), F32))
    out_blk = (None, nh, n, hd)
    out_shape = [jax.ShapeDtypeStruct((b, nh, n, hd), F32)]
    out_specs = [fixed(out_blk, lambda bi, s: (bi, 0, 0, 0))]
    if mode == "win":
        out_shape.append(jax.ShapeDtypeStruct((b, nh, n, LANES), F32))
        out_specs.append(fixed(out_blk, lambda bi, s: (bi, 0, 0, 0)))
    kern = functools.partial(_decode_kernel, mode=mode, n_pg=n_pg, paged=paged, lam_init=lam_init)
    name = "decode_" + mode
    if paged:
        grid_spec = pltpu.PrefetchScalarGridSpec(num_scalar_prefetch=1, grid=(b, steps), in_specs=specs,
                                                 out_specs=out_specs, scratch_shapes=scratch)
        out = pl.pallas_call(kern, grid_spec=grid_spec, out_shape=out_shape, name=name,
                             compiler_params=_params("arbitrary", "arbitrary"))(page_table, *args)
    else:
        out = pl.pallas_call(kern, grid=(b, steps), in_specs=specs, out_specs=out_specs, out_shape=out_shape,
                             scratch_shapes=scratch, name=name,
                             compiler_params=_params("arbitrary", "arbitrary"))(*args)
    return out if mode == "win" else out[0]


def _merge_groups_kernel(*refs):
    o_refs, lse_refs, out_ref = refs[:N_GROUPS], refs[N_GROUPS:2 * N_GROUPS], refs[2 * N_GROUPS]
    lses = [r[...] for r in lse_refs]
    m = jnp.maximum(jnp.maximum(lses[0], lses[1]), lses[2])
    es = [jnp.exp(v - m) for v in lses]
    inv = 1.0 / (es[0] + es[1] + es[2])
    out_ref[...] = sum(es[g] * o_refs[g][...] for g in range(N_GROUPS)) * inv


def _merge_groups(outs, lses):
    shape = outs[0].shape
    spec = pl.BlockSpec(shape, lambda i: (0,) * len(shape))
    return pl.pallas_call(
        _merge_groups_kernel, grid=(1,), in_specs=[spec] * (2 * N_GROUPS), out_specs=spec,
        out_shape=jax.ShapeDtypeStruct(shape, F32), compiler_params=_params("arbitrary"),
        name="merge_groups",
    )(*outs, *lses)


def _paged_mean_kernel(pt_ref, *refs, n_pages, pages_per_blk):
    o_ref = refs[n_pages]
    for j in range(n_pages // pages_per_blk):
        tot = sum(jnp.sum(refs[j * pages_per_blk + r][...], axis=0) for r in range(pages_per_blk))
        o_ref[j] = tot * (1.0 / MOBA_BLOCK)


def _paged_block_mean(cache, layer, page_table):
    b, total_pages = page_table.shape
    tk, _, nh, hd = cache.shape[2:]
    pages_per_blk = MOBA_BLOCK // tk
    n_pages = 8
    steps = total_pages // n_pages
    specs = [pl.BlockSpec((None, None, tk, None, nh, hd), functools.partial(
        lambda bi, s, pt, r: (layer, pt[bi, s * n_pages + r], 0, 0, 0, 0), r=r)) for r in range(n_pages)]
    return pl.pallas_call(
        functools.partial(_paged_mean_kernel, n_pages=n_pages, pages_per_blk=pages_per_blk),
        grid_spec=pltpu.PrefetchScalarGridSpec(
            num_scalar_prefetch=1, grid=(b, steps), in_specs=specs,
            out_specs=pl.BlockSpec((None, n_pages // pages_per_blk, nh, hd), lambda bi, s, pt: (bi, s, 0, 0))),
        out_shape=jax.ShapeDtypeStruct((b, total_pages // pages_per_blk, nh, hd), F32),
        compiler_params=_params("arbitrary", "arbitrary"),
        name="paged_block_mean",
    )(page_table, *([cache] * n_pages))


def _shift_window_kernel(cur_ref, nxt_ref, new_ref, o_ref):
    t = cur_ref.shape[0]
    n = new_ref.shape[0]
    o_ref[0:t - n] = cur_ref[n:t]
    last = pl.program_id(2) == pl.num_programs(2) - 1

    @pl.when(last)
    def _():
        o_ref[t - n:t] = new_ref[...]

    @pl.when(jnp.logical_not(last))
    def _():
        o_ref[t - n:t] = nxt_ref[...]


def _shift_window(cache, new):
    layers, b, w = cache.shape[:3]
    n = new.shape[2]
    row = cache.shape[3:]
    t = min(w, 512)
    steps = w // t
    last_head = w // n - 1
    return pl.pallas_call(
        _shift_window_kernel,
        grid=(layers, b, steps),
        in_specs=[pl.BlockSpec((None, None, t) + row, lambda j, bi, i: (j, bi, i, 0, 0, 0)),
                  pl.BlockSpec((None, None, n) + row,
                               lambda j, bi, i: (j, bi, jnp.minimum((i + 1) * (t // n), last_head), 0, 0, 0)),
                  pl.BlockSpec((None, None, n) + row, lambda j, bi, i: (j, bi, 0, 0, 0, 0))],
        out_specs=pl.BlockSpec((None, None, t) + row, lambda j, bi, i: (j, bi, i, 0, 0, 0)),
        out_shape=jax.ShapeDtypeStruct(cache.shape, cache.dtype),
        compiler_params=_params("arbitrary", "arbitrary", "arbitrary"),
        name="shift_window",
    )(cache, cache, new)


def _rows_kernel(a_ref, b_ref, o_ref):
    layer = pl.program_id(0)
    nh = a_ref.shape[0]

    @pl.when(layer == 0)
    def _():
        for h in range(nh):
            o_ref[:, h, :] = a_ref[h]

    @pl.when(layer == 1)
    def _():
        for h in range(nh):
            o_ref[:, h, :] = b_ref[h]


def _rows_of_two(kv_a, kv_b, first):
    b, _, nh, l, hd = kv_a.shape
    tl = 512
    nt = l // tl

    def src(which):
        def index(layer, bi, i, kv):
            live = layer == which
            return (jnp.where(live, bi, (1 - which) * (b - 1)), first + jnp.where(live, kv, (1 - which)), 0,
                    jnp.where(live, i, (1 - which) * (nt - 1)), 0)
        return pl.BlockSpec((None, None, nh, tl, hd), index)

    return pl.pallas_call(
        _rows_kernel,
        grid=(2, b, nt, 2),
        in_specs=[src(0), src(1)],
        out_specs=pl.BlockSpec((None, None, tl, None, nh, hd), lambda layer, bi, i, kv: (layer, bi, i, kv, 0, 0)),
        out_shape=jax.ShapeDtypeStruct((2, b, l, 2, nh, hd), F32),
        compiler_params=_params("arbitrary", "arbitrary", "arbitrary", "arbitrary"),
        name="rows_of",
    )(kv_a, kv_b)


def _rope_tables(pos, half, reps=1):
    inv = ROPE_THETA ** (-np.arange(0, 2 * half, 2, dtype=np.float64) / (2 * half))
    ang = np.asarray(pos, np.float64)[:, None] * inv[None, :]
    cos = np.concatenate([np.cos(ang), np.cos(ang)], axis=1)
    sin = np.concatenate([-np.sin(ang), np.sin(ang)], axis=1)
    tile = LANES // (2 * half)
    cos = np.tile(np.tile(cos, (1, tile)), (reps, 1))
    sin = np.tile(np.tile(sin, (1, tile)), (reps, 1))
    return (jnp.asarray(np.stack([cos, np.ones_like(cos)]), F32),
            jnp.asarray(np.stack([sin, np.zeros_like(sin)]), F32))


def _rows_of(kv_layers):
    return jnp.transpose(jnp.stack(kv_layers), (0, 1, 4, 2, 3, 5))


def kernel(x_prompt, x_sample, c_prompt, c_sample, cache_win1_kv, cache_win2_kv, cache_win3_kv, cache_kv,
           page_table, norm1_g, norm2_g, w_mod, b_mod, w_qkv_a, w_o_a, w_qkv_b, w_o_b, lambda_b, subln_g_b,
           w_qkv_c, w_o_c, w_up, w_down, final_g):
    bp, seq, d = x_prompt.shape
    bs, n_new, _ = x_sample.shape
    depth = w_mod.shape[0]
    past_len = page_table.shape[1] * PAGE_SIZE
    assert past_len % MOBA_BLOCK == 0 and seq % WIN_TILE == 0 and n_new == SUBLANES
    assert cache_kv.shape[2] == PAGE_SIZE
    pos_p = np.arange(seq)
    pos_s = past_len + np.arange(n_new)
    tl_p = 1024

    rope_p = {h: _rope_tables(pos_p, h) for h in (HEAD_DIM // 2, DIFF_HALF // 2)}
    rope_s = {h: _rope_tables(pos_s, h, reps=bs) for h in (HEAD_DIM // 2, DIFF_HALF // 2)}

    rows_c = 2 * SUBLANES
    c_all = jnp.concatenate([c_prompt, c_sample, jnp.zeros((rows_c - bp - bs, d), F32)], axis=0)
    mods = _modulation(c_all, w_mod, b_mod)
    win_caches = (cache_win1_kv, cache_win2_kv, cache_win3_kv)

    xp, xs = x_prompt, x_sample
    win_p = [[] for _ in range(N_GROUPS)]
    win_s = [[] for _ in range(N_GROUPS)]
    kv_p, kv_s = [], []
    for i in range(depth):
        kind = i % N_MIXERS
        j = i // N_MIXERS
        mp = [mods[i, :bp, t * d:(t + 1) * d][:, None, :] for t in range(6)]
        ms = [mods[i, bp:bp + bs, t * d:(t + 1) * d][:, None, :] for t in range(6)]
        if kind == 0:
            half = HEAD_DIM // 2
            qkv_p = _qkv(xp, norm1_g[i], mp[1], mp[0], *rope_p[half], w_qkv_a[j], half=half, tl=tl_p)
            qkv_s = _qkv(xs, norm1_g[i], ms[1], ms[0], *rope_s[half], w_qkv_a[j], half=half, tl=n_new)
            o_p = _win_prompt(qkv_p)
            outs_s, lses_s = [], []
            for g, (window, dil) in enumerate(DIL_GROUPS):
                keep = min(window, seq)
                win_p[g].append(qkv_p[:, 3 * g + 1:3 * g + 3, :, seq - keep:])
                o_g, lse_g = _decode_attn(qkv_s, 3 * g, win_caches[g], j, mode="win", dil=dil, window=window)
                outs_s.append(o_g)
                lses_s.append(lse_g)
                win_s[g].append(qkv_s[:, 3 * g + 1:3 * g + 3])
            o_s = _merge_groups(outs_s, lses_s)
            w_o = w_o_a[j]
        else:
            slot = i - (i + 2) // N_MIXERS
            if kind == 1:
                half = DIFF_HALF // 2
                w_qkv, w_o = w_qkv_b[j], w_o_b[j]
            else:
                half = HEAD_DIM // 2
                w_qkv, w_o = w_qkv_c[j], w_o_c[j]
            qkv_p = _qkv(xp, norm1_g[i], mp[1], mp[0], *rope_p[half], w_qkv, half=half, tl=tl_p)
            qkv_s = _qkv(xs, norm1_g[i], ms[1], ms[0], *rope_s[half], w_qkv, half=half, tl=n_new)
            if kind == 1:
                lam_init = 0.8 - 0.6 * math.exp(-0.3 * i)
                o_p = _flash_prompt(qkv_p, diff=True, lam=lambda_b[j], gsub=subln_g_b[j], lam_init=lam_init)
                o_s = _decode_attn(qkv_s, 0, cache_kv, slot, mode="diff", page_table=page_table,
                                   lam=lambda_b[j], gsub=subln_g_b[j], lam_init=lam_init)
            else:
                o_p = _flash_prompt(qkv_p, diff=False, kmean=_block_mean_prompt(qkv_p))
                o_s = _decode_attn(qkv_s, 0, cache_kv, slot, mode="moba", page_table=page_table,
                                   kmean=_paged_block_mean(cache_kv, slot, page_table))
            kv_p.append(qkv_p)
            kv_s.append(qkv_s[:, 1:3])
        xp = _proj(o_p, w_o, mp[2], xp, tl=tl_p)
        xs = _proj(o_s, w_o, ms[2], xs, tl=n_new)
        xp = _mlp(xp, norm2_g[i], mp[4], mp[3], mp[5], w_up[i], w_down[i], tl=tl_p)
        xs = _mlp(xs, norm2_g[i], ms[4], ms[3], ms[5], w_up[i], w_down[i], tl=n_new)
    y_prompt = _final_norm(xp, final_g)
    y_sample = _final_norm(xs, final_g)
    new_win_s = [_shift_window(win_caches[g], _rows_of(win_s[g])) for g in range(N_GROUPS)]
    if len(kv_p) == 2:
        new_kv_p = _rows_of_two(kv_p[0], kv_p[1], 1)
    else:
        new_kv_p = _rows_of([t[:, 1:3] for t in kv_p])
    return (y_prompt, y_sample,
            _rows_of(win_p[0]), _rows_of(win_p[1]), _rows_of(win_p[2]), new_kv_p,
            new_win_s[0], new_win_s[1], new_win_s[2], _rows_of(kv_s))
```

```python
import functools
import math

import numpy as np
import jax
import jax.numpy as jnp
from jax import lax
from jax.experimental import pallas as pl
from jax.experimental.pallas import tpu as pltpu

F32 = jnp.float32
BF16 = jnp.bfloat16

HEAD_DIM = 128
DIL_GROUPS = ((128, 1), (512, 4), (2048, 16))
N_GROUPS = len(DIL_GROUPS)
DIFF_HALF = HEAD_DIM // 2
MOBA_BLOCK = 256
MOBA_TOPK = 3
PAGE_SIZE = 128
ROPE_THETA = 10000.0
NORM_EPS = 1e-6
SUBLN_EPS = 1e-5
N_MIXERS = 3

LANES = 128
SUBLANES = 8
NEG = -1e30
LOG2E = math.log2(math.e)
VMEM_LIMIT = 56 * 1024 * 1024
PAGES_PER_STEP = 4
ROW_CHUNK = 128
WIN_TILE = 2048
WIN_UNROLL = 4
FLASH_CHUNK = 1024
FLASH_ROWS = 128


def _params(*sem):
    return pltpu.CompilerParams(dimension_semantics=sem, vmem_limit_bytes=VMEM_LIMIT)


def _dot(a, b):
    return jnp.dot(a, b, preferred_element_type=F32)


def _dot_nt(a, b):
    return lax.dot_general(a, b, (((1,), (1,)), ((), ())), preferred_element_type=F32)


def _split_bf16(x):
    hi = x.astype(BF16)
    lo = (x - hi.astype(F32)).astype(BF16)
    return hi, lo


def _dot_nt_3pass(a, b):
    ah, al = _split_bf16(a)
    bh, bl = _split_bf16(b)
    return _dot_nt(ah, bh) + _dot_nt(ah, bl) + _dot_nt(al, bh)


def _norm_mod(x, g, scale, shift):
    y = x * lax.rsqrt(jnp.mean(x * x, axis=-1, keepdims=True) + NORM_EPS) * g
    return y * (1.0 + scale) + shift


def _for_row_chunks(bb, tl, body):
    if bb > 1 or tl <= ROW_CHUNK:
        body(slice(None))
        return

    def step(c, carry):
        body(pl.ds(pl.multiple_of(c * ROW_CHUNK, ROW_CHUNK), ROW_CHUNK))
        return carry

    lax.fori_loop(0, tl // ROW_CHUNK, step, 0)


def _store_norm_mod(x_ref, g_ref, sc_ref, sh_ref, h_s):
    bb, tl, d = x_ref.shape

    def body(rows):
        h = _norm_mod(x_ref[:, rows, :], g_ref[...], sc_ref[...], sh_ref[...])
        h_s[rows, :] = h.reshape(-1, d).astype(BF16)

    _for_row_chunks(bb, tl, body)


def _batch_block(b, l, tl):
    return b if tl == l and b * l <= 1024 else 1


def _mod_kernel(c_ref, w_ref, b_ref, o_ref):
    c = c_ref[...]
    a = (c / (1.0 + jnp.exp(-c))).astype(BF16)
    o_ref[...] = _dot(a, w_ref[...].astype(BF16)) + b_ref[...]


def _modulation(c_all, w_mod, b_mod):
    depth, d, n = w_mod.shape
    rows = c_all.shape[0]
    tn = 1024
    return pl.pallas_call(
        _mod_kernel,
        grid=(depth, n // tn),
        in_specs=[pl.BlockSpec((rows, d), lambda i, j: (0, 0)),
                  pl.BlockSpec((None, d, tn), lambda i, j: (i, 0, j)),
                  pl.BlockSpec((None, 1, tn), lambda i, j: (i, 0, j))],
        out_specs=pl.BlockSpec((None, rows, tn), lambda i, j: (i, 0, j)),
        out_shape=jax.ShapeDtypeStruct((depth, rows, n), F32),
        compiler_params=_params("arbitrary", "arbitrary"),
        name="modulation",
    )(c_all, w_mod, b_mod.reshape(depth, 1, n))


def _rope_chunk(x, cos, sin, half):
    if 2 * half == LANES:
        rot = pltpu.roll(x, half, 1)
    else:
        lane = lax.broadcasted_iota(jnp.int32, x.shape, 1)
        rot = jnp.where(lane % (2 * half) < half, pltpu.roll(x, LANES - half, 1), pltpu.roll(x, half, 1))
    return x * cos + rot * sin


def _qkv_kernel(x_ref, g_ref, sc_ref, sh_ref, cos_ref, sin_ref, w_ref, o_ref, h_s, *, half):
    n = pl.program_id(2)
    bb, tl, d = x_ref.shape
    tn = w_ref.shape[1]

    @pl.when(n == 0)
    def _():
        _store_norm_mod(x_ref, g_ref, sc_ref, sh_ref, h_s)

    acc = _dot(h_s[...], w_ref[...].astype(BF16))
    cos = cos_ref[...]
    sin = sin_ref[...]
    for c in range(tn // LANES):
        chunk = _rope_chunk(acc[:, c * LANES:(c + 1) * LANES], cos, sin, half)
        o_ref[:, c, :, :] = chunk.reshape(bb, tl, LANES)


def _qkv(x, g, scale, shift, cos, sin, w, *, half, tl):
    b, l, d = x.shape
    n = w.shape[1]
    nh = d // HEAD_DIM
    tn = 512
    hpt = tn // HEAD_DIM
    tpk = d // tn
    bb = _batch_block(b, l, tl)
    table = pl.BlockSpec((None, bb * tl, LANES), lambda i, j, k: (((k // tpk) % 3) // 2, j, 0))
    return pl.pallas_call(
        functools.partial(_qkv_kernel, half=half),
        grid=(b // bb, l // tl, n // tn),
        in_specs=[pl.BlockSpec((bb, tl, d), lambda i, j, k: (i, j, 0)),
                  pl.BlockSpec((1, d), lambda i, j, k: (0, 0)),
                  pl.BlockSpec((bb, 1, d), lambda i, j, k: (i, 0, 0)),
                  pl.BlockSpec((bb, 1, d), lambda i, j, k: (i, 0, 0)),
                  table, table,
                  pl.BlockSpec((d, tn), lambda i, j, k: (0, k))],
        out_specs=pl.BlockSpec((bb, None, hpt, tl, HEAD_DIM), lambda i, j, k: (i, k // tpk, k % tpk, j, 0)),
        out_shape=jax.ShapeDtypeStruct((b, n // d, nh, l, HEAD_DIM), F32),
        scratch_shapes=[pltpu.VMEM((bb * tl, d), BF16)],
        compiler_params=_params("arbitrary", "arbitrary", "arbitrary"),
        name="qkv",
    )(x, g.reshape(1, d), scale, shift, cos, sin, w)


def _proj_kernel(a_ref, w_ref, gate_ref, res_ref, o_ref, a_s):
    n = pl.program_id(2)
    bb, nh, tl, hd = a_ref.shape
    tn = w_ref.shape[1]

    @pl.when(n == 0)
    def _():
        def body(rows):
            for h in range(nh):
                a_s[rows, h * hd:(h + 1) * hd] = a_ref[:, h, rows, :].reshape(-1, hd).astype(BF16)

        _for_row_chunks(bb, tl, body)

    acc = _dot(a_s[...], w_ref[...].astype(BF16))
    o_ref[...] = res_ref[...] + gate_ref[...] * acc.reshape(bb, tl, tn)


def _proj(a, w, gate, res, *, tl):
    b, l, d = res.shape
    nh = a.shape[1]
    tn = 512
    bb = _batch_block(b, l, tl)
    return pl.pallas_call(
        _proj_kernel,
        grid=(b // bb, l // tl, d // tn),
        in_specs=[pl.BlockSpec((bb, nh, tl, HEAD_DIM), lambda i, j, k: (i, 0, j, 0)),
                  pl.BlockSpec((d, tn), lambda i, j, k: (0, k)),
                  pl.BlockSpec((bb, 1, tn), lambda i, j, k: (i, 0, k)),
                  pl.BlockSpec((bb, tl, tn), lambda i, j, k: (i, j, k))],
        out_specs=pl.BlockSpec((bb, tl, tn), lambda i, j, k: (i, j, k)),
        out_shape=jax.ShapeDtypeStruct((b, l, d), F32),
        scratch_shapes=[pltpu.VMEM((bb * tl, d), BF16)],
        compiler_params=_params("arbitrary", "arbitrary", "arbitrary"),
        name="proj",
    )(a, w, gate, res)


def _mlp_kernel(x_ref, g_ref, sc_ref, sh_ref, gate_ref, wu_ref, wd_ref, o_ref, h_s):
    f = pl.program_id(2)
    bb, tl, d = x_ref.shape
    nchunk = 512

    @pl.when(f == 0)
    def _():
        _store_norm_mod(x_ref, g_ref, sc_ref, sh_ref, h_s)
        o_ref[...] = jnp.zeros_like(o_ref)

    a = jnp.maximum(_dot(h_s[...], wu_ref[...].astype(BF16)), 0.0)
    a = (a * a).astype(BF16)
    for c in range(d // nchunk):
        sl = slice(c * nchunk, (c + 1) * nchunk)
        o_ref[:, :, sl] += _dot(a, wd_ref[:, sl].astype(BF16)).reshape(bb, tl, nchunk)

    @pl.when(f == pl.num_programs(2) - 1)
    def _():
        o_ref[...] = x_ref[...] + gate_ref[...] * o_ref[...]


def _mlp(x, g, scale, shift, gate, w_up, w_down, *, tl):
    b, l, d = x.shape
    ff = w_up.shape[1]
    bb = _batch_block(b, l, tl)
    big = bb * tl >= 1024
    tf = 512
    vec = pl.BlockSpec((bb, 1, d), lambda i, j, k: (i, 0, 0))
    x_mode = dict(pipeline_mode=pl.Buffered(1)) if big else {}
    return pl.pallas_call(
        _mlp_kernel,
        grid=(b // bb, l // tl, ff // tf),
        in_specs=[pl.BlockSpec((bb, tl, d), lambda i, j, k: (i, j, 0), **x_mode),
                  pl.BlockSpec((1, d), lambda i, j, k: (0, 0)),
                  vec, vec, vec,
                  pl.BlockSpec((d, tf), lambda i, j, k: (0, k)),
                  pl.BlockSpec((tf, d), lambda i, j, k: (k, 0))],
        out_specs=pl.BlockSpec((bb, tl, d), lambda i, j, k: (i, j, 0), **x_mode),
        out_shape=jax.ShapeDtypeStruct((b, l, d), F32),
        scratch_shapes=[pltpu.VMEM((bb * tl, d), BF16)],
        compiler_params=_params("arbitrary", "arbitrary", "arbitrary"),
        name="mlp",
    )(x, g.reshape(1, d), scale, shift, gate, w_up, w_down)


def _final_norm_kernel(x_ref, g_ref, o_ref):
    x = x_ref[...]
    o_ref[...] = x * lax.rsqrt(jnp.mean(x * x, axis=-1, keepdims=True) + NORM_EPS) * g_ref[...]


def _final_norm(x, g):
    b, l, d = x.shape
    tl = min(l, 512)
    return pl.pallas_call(
        _final_norm_kernel,
        grid=(b, l // tl),
        in_specs=[pl.BlockSpec((1, tl, d), lambda i, j: (i, j, 0)),
                  pl.BlockSpec((1, d), lambda i, j: (0, 0))],
        out_specs=pl.BlockSpec((1, tl, d), lambda i, j: (i, j, 0)),
        out_shape=jax.ShapeDtypeStruct((b, l, d), F32),
        compiler_params=_params("arbitrary", "arbitrary"),
        name="final_norm",
    )(x, g.reshape(1, d))


def _online_update(s, v, m_s, l_s, acc_s, exp=jnp.exp):
    m_prev = m_s[...]
    m_new = jnp.maximum(m_prev, jnp.max(s, axis=1, keepdims=True))
    alpha = exp(m_prev - m_new)
    p = exp(s - m_new)
    l_s[...] = alpha * l_s[...] + jnp.sum(p, axis=1, keepdims=True)
    acc_s[...] = alpha * acc_s[...] + _dot(p.astype(BF16), v)
    m_s[...] = m_new


def _lane_column(x, j):
    lane = lax.broadcasted_iota(jnp.int32, x.shape, 1)
    return jnp.sum(jnp.where(lane == j, x, 0.0), axis=1, keepdims=True)


def _topk_bias(gate, n_valid, k):
    lane = lax.broadcasted_iota(jnp.int32, gate.shape, 1).astype(F32)
    g = jnp.where(lane < n_valid, gate, NEG)
    bias = jnp.full(gate.shape, NEG, F32)
    for _ in range(k):
        mx = jnp.max(g, axis=1, keepdims=True)
        first = jnp.min(jnp.where(g == mx, lane, float(gate.shape[1])), axis=1, keepdims=True)
        pick = (lane == first) & (mx > 0.5 * NEG)
        bias = jnp.where(pick, 0.0, bias)
        g = jnp.where(lane == first, NEG, g)
    return bias


def _diff_lambda(lam, lam_init):
    return (jnp.exp(jnp.sum(lam[0:1] * lam[1:2], axis=1, keepdims=True))
            - jnp.exp(jnp.sum(lam[2:3] * lam[3:4], axis=1, keepdims=True)) + lam_init)


def _sub_norm(o, gsub, lam_init):
    return o * lax.rsqrt(jnp.mean(o * o, axis=-1, keepdims=True) + SUBLN_EPS) * gsub * (1.0 - lam_init)


def _win_prompt_kernel(*refs, nw):
    in_refs, (o_ref, kcat_s, vcat_s, og_s, lse_s) = refs[:5 * N_GROUPS], refs[5 * N_GROUPS:]
    t = pl.program_id(2)
    tile = o_ref.shape[0]
    scale = HEAD_DIM ** -0.5 * LOG2E
    r = lax.broadcasted_iota(jnp.int32, (nw, 2 * nw), 0)
    c = lax.broadcasted_iota(jnp.int32, (nw, 2 * nw), 1)
    dist = nw + r - c
    band = (dist >= 0) & (dist <= nw)

    for g, (window, dil) in enumerate(DIL_GROUPS):
        q_ref, kc_ref, kp_ref, vc_ref, vp_ref = in_refs[5 * g:5 * g + 5]
        kcat_s[0:window, :] = kp_ref[...]
        kcat_s[window:window + tile, :] = kc_ref[...]
        vcat_s[0:window, :] = vp_ref[...]
        vcat_s[window:window + tile, :] = vc_ref[...]

        def blocks(ug, carry, g=g, dil=dil, window=window, q_ref=q_ref):
            work = []
            for i in range(WIN_UNROLL):
                u = ug * WIN_UNROLL + i
                start = u % dil + (u // dil) * window
                if dil == 1:
                    rows_q = pl.ds(start, nw)
                    rows_k = pl.ds(start, 2 * nw)
                else:
                    rows_q = pl.ds(start, nw, stride=dil)
                    rows_k = pl.ds(start, 2 * nw, stride=dil)
                q = (q_ref[rows_q, :] * scale).astype(BF16)
                k = kcat_s[rows_k, :].astype(BF16)
                first_key = jnp.where((t > 0) | (u >= dil), 0, nw)
                s = _dot_nt(q, k) + jnp.where(band & (c >= first_key), 0.0, NEG)
                work.append((rows_q, rows_k, s))
            for rows_q, rows_k, s in work:
                v = vcat_s[rows_k, :].astype(BF16)
                m = jnp.max(s, axis=1, keepdims=True)
                p = jnp.exp2(s - m)
                l = jnp.sum(p, axis=1, keepdims=True)
                og_s[g, rows_q, :] = _dot(p.astype(BF16), v) * (1.0 / l)
                lse_s[g, rows_q, :] = jnp.broadcast_to(m + jnp.log2(l), (nw, LANES))
            return carry

        lax.fori_loop(0, tile // nw // WIN_UNROLL, blocks, 0)

    def merge(rows):
        lses = [lse_s[g, rows, :] for g in range(N_GROUPS)]
        m = jnp.maximum(jnp.maximum(lses[0], lses[1]), lses[2])
        es = [jnp.exp2(v - m) for v in lses]
        inv = 1.0 / (es[0] + es[1] + es[2])
        o_ref[rows, :] = sum(es[g] * og_s[g, rows, :] for g in range(N_GROUPS)) * inv

    _for_row_chunks(1, tile, merge)


def _win_prompt(qkv):
    b, _, nh, l, hd = qkv.shape
    tile = WIN_TILE
    nw = DIL_GROUPS[0][0] // DIL_GROUPS[0][1]
    specs = []
    for g, (window, dil) in enumerate(DIL_GROUPS):
        assert window // dil == nw and tile % window == 0

        def cur(t, g=g):
            return pl.BlockSpec((None, None, None, tile, hd), lambda bi, h, i: (bi, 3 * g + t, h, i, 0))

        def prev(t, g=g, per=tile // window, window=window):
            return pl.BlockSpec((None, None, None, window, hd),
                                lambda bi, h, i: (bi, 3 * g + t, h, jnp.maximum(i * per - 1, 0), 0))

        specs += [cur(0), cur(1), prev(1), cur(2), prev(2)]
    max_w = DIL_GROUPS[-1][0]
    return pl.pallas_call(
        functools.partial(_win_prompt_kernel, nw=nw),
        grid=(b, nh, l // tile),
        in_specs=specs,
        out_specs=pl.BlockSpec((None, None, tile, hd), lambda bi, h, i: (bi, h, i, 0)),
        out_shape=jax.ShapeDtypeStruct((b, nh, l, hd), F32),
        scratch_shapes=[pltpu.VMEM((max_w + tile, hd), F32), pltpu.VMEM((max_w + tile, hd), F32),
                        pltpu.VMEM((N_GROUPS, tile, hd), F32), pltpu.VMEM((N_GROUPS, tile, LANES), F32)],
        compiler_params=_params("arbitrary", "arbitrary", "arbitrary"),
        name="win_prompt",
    )(*([qkv] * (5 * N_GROUPS)))


def _flash_prompt_kernel(*refs, diff, lam_init, tq):
    if diff:
        lam_ref, gsub_ref, q_ref, k_ref, v_ref, o_ref, qs_s, m_s, l_s, acc_s = refs
    else:
        km_ref, q_ref, k_ref, v_ref, o_ref, qs_s, m_s, l_s, acc_s = refs
    qi = pl.program_id(2)
    rows = qs_s.shape[0]
    lane = lax.broadcasted_iota(jnp.int32, (tq, HEAD_DIM), 1)
    q = q_ref[...]
    if diff:
        qsc = q * (DIFF_HALF ** -0.5 * LOG2E)
        qs_s[...] = jnp.concatenate([jnp.where(lane < DIFF_HALF, qsc, 0.0),
                                     jnp.where(lane >= DIFF_HALF, qsc, 0.0)], axis=0).astype(BF16)
    else:
        qs_s[...] = (q * (HEAD_DIM ** -0.5 * LOG2E)).astype(BF16)
        own = (qi * tq + lax.broadcasted_iota(jnp.int32, (tq, 1), 0)) // MOBA_BLOCK
        sel_bias = _topk_bias(_dot_nt_3pass(q, km_ref[...]), own.astype(F32), MOBA_TOPK)
    m_s[...] = jnp.full(m_s.shape, NEG, F32)
    l_s[...] = jnp.zeros(l_s.shape, F32)
    acc_s[...] = jnp.zeros(acc_s.shape, F32)

    chunk = FLASH_CHUNK
    diag = qi // (chunk // tq)

    def selection_bias(ci):
        per_chunk = chunk // MOBA_BLOCK
        cols = [jnp.broadcast_to(_lane_column(sel_bias, ci * per_chunk + t), (tq, MOBA_BLOCK))
                for t in range(per_chunk)]
        return jnp.concatenate(cols, axis=1)

    def consume(ci, bias):
        rows_k = pl.ds(pl.multiple_of(ci * chunk, chunk), chunk)
        k = k_ref[rows_k, :].astype(BF16)
        v = v_ref[rows_k, :].astype(BF16)
        sub = FLASH_ROWS
        scores = [_dot_nt(qs_s[r0:r0 + sub, :], k) for r0 in range(0, rows, sub)]
        for r0, s in zip(range(0, rows, sub), scores):
            rs = slice(r0, r0 + sub)
            if bias is not None:
                s = s + bias[r0 % tq:r0 % tq + sub]
            _online_update(s, v, m_s.at[rs, :], l_s.at[rs, :], acc_s.at[rs, :], exp=jnp.exp2)

    def past_chunk(ci, carry):
        consume(ci, None if diff else selection_bias(ci))
        return carry

    lax.fori_loop(0, diag, past_chunk, 0)

    q_pos = qi * tq + lax.broadcasted_iota(jnp.int32, (tq, chunk), 0)
    k_pos = diag * chunk + lax.broadcasted_iota(jnp.int32, (tq, chunk), 1)
    bias = jnp.where(k_pos <= q_pos, 0.0, NEG)
    if not diff:
        bias = jnp.where(k_pos // MOBA_BLOCK >= q_pos // MOBA_BLOCK, bias, selection_bias(diag))
    consume(diag, bias)

    o = acc_s[...] * (1.0 / l_s[...])
    if diff:
        o = o[:tq] - _diff_lambda(lam_ref[...], lam_init) * o[tq:]
        o = _sub_norm(o, gsub_ref[...], lam_init)
    o_ref[...] = o


def _flash_prompt(qkv, *, diff, lam=None, gsub=None, lam_init=0.0, kmean=None):
    b, _, nh, l, hd = qkv.shape
    tq = MOBA_BLOCK if diff else 2 * MOBA_BLOCK
    assert l % FLASH_CHUNK == 0 and FLASH_CHUNK % tq == 0
    rows = 2 * tq if diff else tq
    qkv_specs = [pl.BlockSpec((None, None, None, tq, hd), lambda bi, h, i: (bi, 0, h, i, 0)),
                 pl.BlockSpec((None, None, None, l, hd), lambda bi, h, i: (bi, 1, h, 0, 0)),
                 pl.BlockSpec((None, None, None, l, hd), lambda bi, h, i: (bi, 2, h, 0, 0))]
    if diff:
        extra = [lam, gsub.reshape(1, hd)]
        extra_specs = [pl.BlockSpec(lam.shape, lambda bi, h, i: (0, 0)),
                       pl.BlockSpec((1, hd), lambda bi, h, i: (0, 0))]
    else:
        extra = [kmean]
        extra_specs = [pl.BlockSpec((None, None, kmean.shape[2], hd), lambda bi, h, i: (bi, h, 0, 0))]
    return pl.pallas_call(
        functools.partial(_flash_prompt_kernel, diff=diff, lam_init=lam_init, tq=tq),
        grid=(b, nh, l // tq),
        in_specs=extra_specs + qkv_specs,
        out_specs=pl.BlockSpec((None, None, tq, hd), lambda bi, h, i: (bi, h, i, 0)),
        out_shape=jax.ShapeDtypeStruct((b, nh, l, hd), F32),
        scratch_shapes=[pltpu.VMEM((rows, hd), BF16), pltpu.VMEM((rows, 1), F32),
                        pltpu.VMEM((rows, 1), F32), pltpu.VMEM((rows, hd), F32)],
        compiler_params=_params("arbitrary", "arbitrary", "arbitrary"),
        name="flash_diff" if diff else "flash_moba",
    )(*extra, qkv, qkv, qkv)


def _block_mean_kernel(k_ref, o_ref):
    l, hd = k_ref.shape
    o_ref[...] = jnp.sum(k_ref[...].reshape(l // MOBA_BLOCK, MOBA_BLOCK, hd), axis=1) * (1.0 / MOBA_BLOCK)


def _block_mean_prompt(qkv):
    b, _, nh, l, hd = qkv.shape
    nb = l // MOBA_BLOCK
    return pl.pallas_call(
        _block_mean_kernel,
        grid=(b, nh),
        in_specs=[pl.BlockSpec((None, None, None, l, hd), lambda bi, h: (bi, 1, h, 0, 0))],
        out_specs=pl.BlockSpec((None, None, nb, hd), lambda bi, h: (bi, h, 0, 0)),
        out_shape=jax.ShapeDtypeStruct((b, nh, nb, hd), F32),
        compiler_params=_params("arbitrary", "arbitrary"),
        name="block_mean",
    )(qkv)


def _decode_kernel(*refs, mode, n_pg, paged, lam_init):
    refs = list(refs)
    if paged:
        refs.pop(0)
    q_ref = refs.pop(0)
    pg_refs = [refs.pop(0) for _ in range(n_pg)]
    kn_ref, vn_ref = refs.pop(0), refs.pop(0)
    bias_ref, bnew_ref = refs.pop(0), refs.pop(0)
    km_ref = refs.pop(0) if mode == "moba" else None
    if mode == "diff":
        lam_ref, gsub_ref = refs.pop(0), refs.pop(0)
    o_ref = refs.pop(0)
    lse_ref = refs.pop(0) if mode == "win" else None
    q_s, kf_s, vf_s, m_s, l_s, acc_s = refs[:6]
    sel_s = refs[6] if mode == "moba" else None

    s_id = pl.program_id(1)
    nh, n, hd = q_ref.shape
    rows = q_s.shape[0]
    half = rows // 2
    hpb = nh // 2
    tk = pg_refs[0].shape[0]
    scale = (DIFF_HALF if mode == "diff" else HEAD_DIM) ** -0.5

    @pl.when(s_id == 0)
    def _():
        lane = lax.broadcasted_iota(jnp.int32, (n, hd), 1)
        pieces = []
        for h in range(nh):
            qh = q_ref[h] * scale
            if mode == "diff":
                pieces += [jnp.where(lane < DIFF_HALF, qh, 0.0), jnp.where(lane >= DIFF_HALF, qh, 0.0)]
            else:
                pieces.append(qh)
        q_s[...] = jnp.concatenate(pieces, axis=0).astype(BF16)
        m_s[...] = jnp.full(m_s.shape, NEG, F32)
        l_s[...] = jnp.zeros(l_s.shape, F32)
        acc_s[...] = jnp.zeros(acc_s.shape, F32)
        if mode == "moba":
            n_blk = km_ref.shape[0] // nh
            for h in range(nh):
                gate = _dot_nt_3pass(q_ref[h], km_ref[pl.ds(h, n_blk, stride=nh), :])
                sel_s[h * n:(h + 1) * n, :] = _topk_bias(gate, n_blk, MOBA_TOPK)

    for hb in range(2):
        hs = slice(hb * hpb, (hb + 1) * hpb)
        rs = slice(hb * half, (hb + 1) * half)
        for r in range(n_pg):
            dst = slice(r * tk * hpb, (r + 1) * tk * hpb)
            kf_s[dst, :] = pg_refs[r][:, 0, hs, :].reshape(tk * hpb, hd).astype(BF16)
            vf_s[dst, :] = pg_refs[r][:, 1, hs, :].reshape(tk * hpb, hd).astype(BF16)
        s = _dot_nt(q_s[rs, :], kf_s[...]) + bias_ref[...]
        if mode == "moba":
            pages_per_blk = MOBA_BLOCK // tk
            cols = []
            for r in range(n_pg // pages_per_blk):
                col = _lane_column(sel_s[rs, :], s_id * (n_pg // pages_per_blk) + r)
                cols.append(jnp.broadcast_to(col, (half, MOBA_BLOCK * hpb)))
            s = s + jnp.concatenate(cols, axis=1)
        _online_update(s, vf_s[...], m_s.at[rs, :], l_s.at[rs, :], acc_s.at[rs, :])

    @pl.when(s_id == pl.num_programs(1) - 1)
    def _():
        pad = jnp.zeros((LANES - hpb * n, hd), F32)
        for hb in range(2):
            hs = slice(hb * hpb, (hb + 1) * hpb)
            rs = slice(hb * half, (hb + 1) * half)
            kn = jnp.concatenate([kn_ref[hs].reshape(hpb * n, hd), pad], axis=0).astype(BF16)
            vn = jnp.concatenate([vn_ref[hs].reshape(hpb * n, hd), pad], axis=0).astype(BF16)
            _online_update(_dot_nt(q_s[rs, :], kn) + bnew_ref[...], vn,
                           m_s.at[rs, :], l_s.at[rs, :], acc_s.at[rs, :])
        lsum = l_s[...]
        o = acc_s[...] * (1.0 / lsum)
        if mode == "diff":
            lam_full = _diff_lambda(lam_ref[...], lam_init)
            for h in range(nh):
                oh = o[2 * h * n:(2 * h + 1) * n] - lam_full * o[(2 * h + 1) * n:(2 * h + 2) * n]
                o_ref[h] = _sub_norm(oh, gsub_ref[...], lam_init)
        else:
            o_ref[...] = o.reshape(nh, n, hd)
        if mode == "win":
            lse_ref[...] = jnp.broadcast_to(m_s[...] + jnp.log(lsum), (rows, LANES)).reshape(nh, n, LANES)


def _head_match_bias(half_rows, rows_per_head, n_keys, hpb):
    r = np.arange(half_rows)[:, None] // rows_per_head
    c = np.arange(n_keys * hpb)[None, :] % hpb
    return np.where(r == c, 0.0, NEG)


def _decode_attn(qkv, gt, cache, layer, *, mode, page_table=None, dil=1, window=None,
                 kmean=None, lam=None, gsub=None, lam_init=0.0):
    b, _, nh, n, hd = qkv.shape
    hpb = nh // 2
    comps = 2 if mode == "diff" else 1
    rows = nh * comps * n
    half = rows // 2
    paged = page_table is not None
    tk = PAGE_SIZE
    total_pages = page_table.shape[1] if paged else cache.shape[2] // tk
    n_pg = min(PAGES_PER_STEP, total_pages)
    steps = total_pages // n_pg
    blk = (None, None, tk, 2, nh, hd)
    if paged:
        pg_specs = [pl.BlockSpec(blk, functools.partial(
            lambda bi, s, pt, r: (layer, pt[bi, s * n_pg + r], 0, 0, 0, 0), r=r)) for r in range(n_pg)]
    else:
        pg_specs = [pl.BlockSpec(blk, functools.partial(
            lambda bi, s, r: (layer, bi, s * n_pg + r, 0, 0, 0), r=r)) for r in range(n_pg)]

    def fixed(shape, idx):
        if paged:
            return pl.BlockSpec(shape, lambda bi, s, pt: idx(bi, s))
        return pl.BlockSpec(shape, lambda bi, s: idx(bi, s))

    qi = np.arange(half)[:, None] % n
    bias = _head_match_bias(half, comps * n, total_pages * tk if mode == "win" else n_pg * tk, hpb)
    cn = np.arange(LANES)[None, :]
    new_ok = (cn < hpb * n) & (cn // n == np.arange(half)[:, None] // (comps * n)) & (cn % n <= qi)
    if mode == "win":
        n_buf = total_pages * tk
        dist = n_buf + qi - np.arange(n_buf * hpb)[None, :] // hpb
        bias = np.where((dist % dil == 0) & (dist <= window), bias, NEG)
        new_ok &= (qi - cn % n) % dil == 0
    bias = jnp.asarray(bias, F32)
    bias_new = jnp.asarray(np.where(new_ok, 0.0, NEG), F32)

    head_blk = (None, None, nh, n, hd)
    args = [qkv] + [cache] * n_pg + [qkv, qkv, bias, bias_new]
    specs = [fixed(head_blk, lambda bi, s: (bi, gt, 0, 0, 0))] + pg_specs + [
        fixed(head_blk, lambda bi, s: (bi, gt + 1, 0, 0, 0)),
        fixed(head_blk, lambda bi, s: (bi, gt + 2, 0, 0, 0)),
        fixed((half, n_pg * tk * hpb), (lambda bi, s: (0, s)) if mode == "win" else (lambda bi, s: (0, 0))),
        fixed((half, LANES), lambda bi, s: (0, 0))]
    if mode == "moba":
        n_blk = kmean.shape[1]
        args.append(kmean.reshape(b, n_blk * nh, hd))
        specs.append(fixed((None, n_blk * nh, hd), lambda bi, s: (bi, 0, 0)))
    if mode == "diff":
        args += [lam, gsub.reshape(1, hd)]
        specs += [fixed(lam.shape, lambda bi, s: (0, 0)), fixed((1, hd), lambda bi, s: (0, 0))]
    scratch = [pltpu.VMEM((rows, hd), BF16),
               pltpu.VMEM((n_pg * tk * hpb, hd), BF16), pltpu.VMEM((n_pg * tk * hpb, hd), BF16),
               pltpu.VMEM((rows, 1), F32), pltpu.VMEM((rows, 1), F32), pltpu.VMEM((rows, hd), F32)]
    if mode == "moba":
        scratch.append(pltpu.VMEM((rows, n_blk), F32))
    out_blk = (None, nh, n, hd)
    out_shape = [jax.ShapeDtypeStruct((b, nh, n, hd), F32)]
    out_specs = [fixed(out_blk, lambda bi, s: (bi, 0, 0, 0))]
    if mode == "win":
        out_shape.append(jax.ShapeDtypeStruct((b, nh, n, LANES), F32))
        out_specs.append(fixed(out_blk, lambda bi, s: (bi, 0, 0, 0)))
    kern = functools.partial(_decode_kernel, mode=mode, n_pg=n_pg, paged=paged, lam_init=lam_init)
    name = "decode_" + mode
    if paged:
        grid_spec = pltpu.PrefetchScalarGridSpec(num_scalar_prefetch=1, grid=(b, steps), in_specs=specs,
                                                 out_specs=out_specs, scratch_shapes=scratch)
        out = pl.pallas_call(kern, grid_spec=grid_spec, out_shape=out_shape, name=name,
                             compiler_params=_params("arbitrary", "arbitrary"))(page_table, *args)
    else:
        out = pl.pallas_call(kern, grid=(b, steps), in_specs=specs, out_specs=out_specs, out_shape=out_shape,
                             scratch_shapes=scratch, name=name,
                             compiler_params=_params("arbitrary", "arbitrary"))(*args)
    return out if mode == "win" else out[0]


def _merge_groups_kernel(*refs):
    o_refs, lse_refs, out_ref = refs[:N_GROUPS], refs[N_GROUPS:2 * N_GROUPS], refs[2 * N_GROUPS]
    lses = [r[...] for r in lse_refs]
    m = jnp.maximum(jnp.maximum(lses[0], lses[1]), lses[2])
    es = [jnp.exp(v - m) for v in lses]
    inv = 1.0 / (es[0] + es[1] + es[2])
    out_ref[...] = sum(es[g] * o_refs[g][...] for g in range(N_GROUPS)) * inv


def _merge_groups(outs, lses):
    shape = outs[0].shape
    spec = pl.BlockSpec(shape, lambda i: (0,) * len(shape))
    return pl.pallas_call(
        _merge_groups_kernel, grid=(1,), in_specs=[spec] * (2 * N_GROUPS), out_specs=spec,
        out_shape=jax.ShapeDtypeStruct(shape, F32), compiler_params=_params("arbitrary"),
        name="merge_groups",
    )(*outs, *lses)


def _paged_mean_kernel(pt_ref, *refs, n_pages, pages_per_blk):
    o_ref = refs[n_pages]
    for j in range(n_pages // pages_per_blk):
        tot = sum(jnp.sum(refs[j * pages_per_blk + r][...], axis=0) for r in range(pages_per_blk))
        o_ref[j] = tot * (1.0 / MOBA_BLOCK)


def _paged_block_mean(cache, layer, page_table):
    b, total_pages = page_table.shape
    tk, _, nh, hd = cache.shape[2:]
    pages_per_blk = MOBA_BLOCK // tk
    n_pages = 8
    steps = total_pages // n_pages
    specs = [pl.BlockSpec((None, None, tk, None, nh, hd), functools.partial(
        lambda bi, s, pt, r: (layer, pt[bi, s * n_pages + r], 0, 0, 0, 0), r=r)) for r in range(n_pages)]
    return pl.pallas_call(
        functools.partial(_paged_mean_kernel, n_pages=n_pages, pages_per_blk=pages_per_blk),
        grid_spec=pltpu.PrefetchScalarGridSpec(
            num_scalar_prefetch=1, grid=(b, steps), in_specs=specs,
            out_specs=pl.BlockSpec((None, n_pages // pages_per_blk, nh, hd), lambda bi, s, pt: (bi, s, 0, 0))),
        out_shape=jax.ShapeDtypeStruct((b, total_pages // pages_per_blk, nh, hd), F32),
        compiler_params=_params("arbitrary", "arbitrary"),
        name="paged_block_mean",
    )(page_table, *([cache] * n_pages))


def _shift_window_kernel(cur_ref, nxt_ref, new_ref, o_ref):
    t = cur_ref.shape[0]
    n = new_ref.shape[0]
    o_ref[0:t - n] = cur_ref[n:t]
    last = pl.program_id(2) == pl.num_programs(2) - 1

    @pl.when(last)
    def _():
        o_ref[t - n:t] = new_ref[...]

    @pl.when(jnp.logical_not(last))
    def _():
        o_ref[t - n:t] = nxt_ref[...]


def _shift_window(cache, new):
    layers, b, w = cache.shape[:3]
    n = new.shape[2]
    row = cache.shape[3:]
    t = min(w, 512)
    steps = w // t
    last_head = w // n - 1
    return pl.pallas_call(
        _shift_window_kernel,
        grid=(layers, b, steps),
        in_specs=[pl.BlockSpec((None, None, t) + row, lambda j, bi, i: (j, bi, i, 0, 0, 0)),
                  pl.BlockSpec((None, None, n) + row,
                               lambda j, bi, i: (j, bi, jnp.minimum((i + 1) * (t // n), last_head), 0, 0, 0)),
                  pl.BlockSpec((None, None, n) + row, lambda j, bi, i: (j, bi, 0, 0, 0, 0))],
        out_specs=pl.BlockSpec((None, None, t) + row, lambda j, bi, i: (j, bi, i, 0, 0, 0)),
        out_shape=jax.ShapeDtypeStruct(cache.shape, cache.dtype),
        compiler_params=_params("arbitrary", "arbitrary", "arbitrary"),
        name="shift_window",
    )(cache, cache, new)


def _rows_kernel(a_ref, b_ref, o_ref):
    layer = pl.program_id(0)
    nh = a_ref.shape[0]

    @pl.when(layer == 0)
    def _():
        for h in range(nh):
            o_ref[:, h, :] = a_ref[h]

    @pl.when(layer == 1)
    def _():
        for h in range(nh):
            o_ref[:, h, :] = b_ref[h]


def _rows_of_two(kv_a, kv_b, first, n_rows):
    b, _, nh, seq, hd = kv_a.shape
    l = n_rows
    tl = min(512, l)
    nt = l // tl
    assert (seq - l) % tl == 0
    skip = (seq - l) // tl

    def src(which):
        def index(layer, bi, i, kv):
            live = layer == which
            return (jnp.where(live, bi, (1 - which) * (b - 1)), first + jnp.where(live, kv, (1 - which)), 0,
                    skip + jnp.where(live, i, (1 - which) * (nt - 1)), 0)
        return pl.BlockSpec((None, None, nh, tl, hd), index)

    return pl.pallas_call(
        _rows_kernel,
        grid=(2, b, nt, 2),
        in_specs=[src(0), src(1)],
        out_specs=pl.BlockSpec((None, None, tl, None, nh, hd), lambda layer, bi, i, kv: (layer, bi, i, kv, 0, 0)),
        out_shape=jax.ShapeDtypeStruct((2, b, l, 2, nh, hd), F32),
        compiler_params=_params("arbitrary", "arbitrary", "arbitrary", "arbitrary"),
        name="rows_of",
    )(kv_a, kv_b)


def _rope_tables(pos, half, reps=1):
    inv = ROPE_THETA ** (-np.arange(0, 2 * half, 2, dtype=np.float64) / (2 * half))
    ang = np.asarray(pos, np.float64)[:, None] * inv[None, :]
    cos = np.concatenate([np.cos(ang), np.cos(ang)], axis=1)
    sin = np.concatenate([-np.sin(ang), np.sin(ang)], axis=1)
    tile = LANES // (2 * half)
    cos = np.tile(np.tile(cos, (1, tile)), (reps, 1))
    sin = np.tile(np.tile(sin, (1, tile)), (reps, 1))
    return (jnp.asarray(np.stack([cos, np.ones_like(cos)]), F32),
            jnp.asarray(np.stack([sin, np.zeros_like(sin)]), F32))


def _rows_of(kv_layers):
    return jnp.transpose(jnp.stack(kv_layers), (0, 1, 4, 2, 3, 5))


def kernel(x_prompt, x_sample, c_prompt, c_sample, cache_win1_kv, cache_win2_kv, cache_win3_kv, cache_kv,
           page_table, norm1_g, norm2_g, w_mod, b_mod, w_qkv_a, w_o_a, w_qkv_b, w_o_b, lambda_b, subln_g_b,
           w_qkv_c, w_o_c, w_up, w_down, final_g):
    bp, seq, d = x_prompt.shape
    bs, n_new, _ = x_sample.shape
    depth = w_mod.shape[0]
    past_len = page_table.shape[1] * PAGE_SIZE
    assert past_len % MOBA_BLOCK == 0 and seq % WIN_TILE == 0 and n_new == SUBLANES
    assert cache_kv.shape[2] == PAGE_SIZE
    pos_p = np.arange(seq)
    pos_s = past_len + np.arange(n_new)
    tl_p = 1024

    rope_p = {h: _rope_tables(pos_p, h) for h in (HEAD_DIM // 2, DIFF_HALF // 2)}
    rope_s = {h: _rope_tables(pos_s, h, reps=bs) for h in (HEAD_DIM // 2, DIFF_HALF // 2)}

    rows_c = 2 * SUBLANES
    c_all = jnp.concatenate([c_prompt, c_sample, jnp.zeros((rows_c - bp - bs, d), F32)], axis=0)
    mods = _modulation(c_all, w_mod, b_mod)
    win_caches = (cache_win1_kv, cache_win2_kv, cache_win3_kv)

    xp, xs = x_prompt, x_sample
    win_p = [[] for _ in range(N_GROUPS)]
    win_s = [[] for _ in range(N_GROUPS)]
    kv_p, kv_s = [], []
    for i in range(depth):
        kind = i % N_MIXERS
        j = i // N_MIXERS
        mp = [mods[i, :bp, t * d:(t + 1) * d][:, None, :] for t in range(6)]
        ms = [mods[i, bp:bp + bs, t * d:(t + 1) * d][:, None, :] for t in range(6)]
        if kind == 0:
            half = HEAD_DIM // 2
            qkv_p = _qkv(xp, norm1_g[i], mp[1], mp[0], *rope_p[half], w_qkv_a[j], half=half, tl=tl_p)
            qkv_s = _qkv(xs, norm1_g[i], ms[1], ms[0], *rope_s[half], w_qkv_a[j], half=half, tl=n_new)
            o_p = _win_prompt(qkv_p)
            outs_s, lses_s = [], []
            for g, (window, dil) in enumerate(DIL_GROUPS):
                win_p[g].append(qkv_p)
                o_g, lse_g = _decode_attn(qkv_s, 3 * g, win_caches[g], j, mode="win", dil=dil, window=window)
                outs_s.append(o_g)
                lses_s.append(lse_g)
                win_s[g].append(qkv_s[:, 3 * g + 1:3 * g + 3])
            o_s = _merge_groups(outs_s, lses_s)
            w_o = w_o_a[j]
        else:
            slot = i - (i + 2) // N_MIXERS
            if kind == 1:
                half = DIFF_HALF // 2
                w_qkv, w_o = w_qkv_b[j], w_o_b[j]
            else:
                half = HEAD_DIM // 2
                w_qkv, w_o = w_qkv_c[j], w_o_c[j]
            qkv_p = _qkv(xp, norm1_g[i], mp[1], mp[0], *rope_p[half], w_qkv, half=half, tl=tl_p)
            qkv_s = _qkv(xs, norm1_g[i], ms[1], ms[0], *rope_s[half], w_qkv, half=half, tl=n_new)
            if kind == 1:
                lam_init = 0.8 - 0.6 * math.exp(-0.3 * i)
                o_p = _flash_prompt(qkv_p, diff=True, lam=lambda_b[j], gsub=subln_g_b[j], lam_init=lam_init)
                o_s = _decode_attn(qkv_s, 0, cache_kv, slot, mode="diff", page_table=page_table,
                                   lam=lambda_b[j], gsub=subln_g_b[j], lam_init=lam_init)
            else:
                o_p = _flash_prompt(qkv_p, diff=False, kmean=_block_mean_prompt(qkv_p))
                o_s = _decode_attn(qkv_s, 0, cache_kv, slot, mode="moba", page_table=page_table,
                                   kmean=_paged_block_mean(cache_kv, slot, page_table))
            kv_p.append(qkv_p)
            kv_s.append(qkv_s[:, 1:3])
        xp = _proj(o_p, w_o, mp[2], xp, tl=tl_p)
        xs = _proj(o_s, w_o, ms[2], xs, tl=n_new)
        xp = _mlp(xp, norm2_g[i], mp[4], mp[3], mp[5], w_up[i], w_down[i], tl=tl_p)
        xs = _mlp(xs, norm2_g[i], ms[4], ms[3], ms[5], w_up[i], w_down[i], tl=n_new)
    y_prompt = _final_norm(xp, final_g)
    y_sample = _final_norm(xs, final_g)
    new_win_s = [_shift_window(win_caches[g], _rows_of(win_s[g])) for g in range(N_GROUPS)]
    def prompt_rows(layers, first, keep):
        if len(layers) == 2:
            return _rows_of_two(layers[0], layers[1], first, keep)
        return _rows_of([t[:, first:first + 2, :, seq - keep:] for t in layers])

    new_win_p = [prompt_rows(win_p[g], 3 * g + 1, min(DIL_GROUPS[g][0], seq)) for g in range(N_GROUPS)]
    new_kv_p = prompt_rows(kv_p, 1, seq)
    return (y_prompt, y_sample,
            new_win_p[0], new_win_p[1], new_win_p[2], new_kv_p,
            new_win_s[0], new_win_s[1], new_win_s[2], _rows_of(kv_s))
```

```python
import functools
import math

import numpy as np
import jax
import jax.numpy as jnp
from jax import lax
from jax.experimental import pallas as pl
from jax.experimental.pallas import tpu as pltpu

F32 = jnp.float32
BF16 = jnp.bfloat16

HEAD_DIM = 128
DIL_GROUPS = ((128, 1), (512, 4), (2048, 16))
N_GROUPS = len(DIL_GROUPS)
DIFF_HALF = HEAD_DIM // 2
MOBA_BLOCK = 256
MOBA_TOPK = 3
PAGE_SIZE = 128
ROPE_THETA = 10000.0
NORM_EPS = 1e-6
SUBLN_EPS = 1e-5
N_MIXERS = 3

LANES = 128
SUBLANES = 8
NEG = -1e30
LOG2E = math.log2(math.e)
VMEM_LIMIT = 56 * 1024 * 1024
PAGES_PER_STEP = 4
ROW_CHUNK = 128
WIN_TILE = 2048
WIN_UNROLL = 4
FLASH_CHUNK = 1024
FLASH_ROWS = 128


def _params(*sem):
    return pltpu.CompilerParams(dimension_semantics=sem, vmem_limit_bytes=VMEM_LIMIT)


def _dot(a, b):
    return jnp.dot(a, b, preferred_element_type=F32)


def _dot_nt(a, b):
    return lax.dot_general(a, b, (((1,), (1,)), ((), ())), preferred_element_type=F32)


def _split_bf16(x):
    hi = x.astype(BF16)
    lo = (x - hi.astype(F32)).astype(BF16)
    return hi, lo


def _dot_nt_3pass(a, b):
    ah, al = _split_bf16(a)
    bh, bl = _split_bf16(b)
    return _dot_nt(ah, bh) + _dot_nt(ah, bl) + _dot_nt(al, bh)


def _norm_mod(x, g, scale, shift):
    y = x * lax.rsqrt(jnp.mean(x * x, axis=-1, keepdims=True) + NORM_EPS) * g
    return y * (1.0 + scale) + shift


def _for_row_chunks(bb, tl, body):
    if bb > 1 or tl <= ROW_CHUNK:
        body(slice(None))
        return

    def step(c, carry):
        body(pl.ds(pl.multiple_of(c * ROW_CHUNK, ROW_CHUNK), ROW_CHUNK))
        return carry

    lax.fori_loop(0, tl // ROW_CHUNK, step, 0)


def _store_norm_mod(x_ref, g_ref, sc_ref, sh_ref, h_s):
    bb, tl, d = x_ref.shape

    def body(rows):
        h = _norm_mod(x_ref[:, rows, :], g_ref[...], sc_ref[...], sh_ref[...])
        h_s[rows, :] = h.reshape(-1, d).astype(BF16)

    _for_row_chunks(bb, tl, body)


def _batch_block(b, l, tl):
    return b if tl == l and b * l <= 1024 else 1


def _mod_kernel(c_ref, w_ref, b_ref, o_ref):
    c = c_ref[...]
    a = (c / (1.0 + jnp.exp(-c))).astype(BF16)
    o_ref[...] = _dot(a, w_ref[...].astype(BF16)) + b_ref[...]


def _modulation(c_all, w_mod, b_mod):
    depth, d, n = w_mod.shape
    rows = c_all.shape[0]
    tn = 1024
    return pl.pallas_call(
        _mod_kernel,
        grid=(depth, n // tn),
        in_specs=[pl.BlockSpec((rows, d), lambda i, j: (0, 0)),
                  pl.BlockSpec((None, d, tn), lambda i, j: (i, 0, j)),
                  pl.BlockSpec((None, 1, tn), lambda i, j: (i, 0, j))],
        out_specs=pl.BlockSpec((None, rows, tn), lambda i, j: (i, 0, j)),
        out_shape=jax.ShapeDtypeStruct((depth, rows, n), F32),
        compiler_params=_params("arbitrary", "arbitrary"),
        name="modulation",
    )(c_all, w_mod, b_mod.reshape(depth, 1, n))


def _rope_chunk(x, cos, sin, half):
    if 2 * half == LANES:
        rot = pltpu.roll(x, half, 1)
    else:
        lane = lax.broadcasted_iota(jnp.int32, x.shape, 1)
        rot = jnp.where(lane % (2 * half) < half, pltpu.roll(x, LANES - half, 1), pltpu.roll(x, half, 1))
    return x * cos + rot * sin


def _qkv_kernel(x_ref, g_ref, sc_ref, sh_ref, cos_ref, sin_ref, w_ref, o_ref, h_s, *, half):
    n = pl.program_id(2)
    bb, tl, d = x_ref.shape
    tn = w_ref.shape[1]

    @pl.when(n == 0)
    def _():
        _store_norm_mod(x_ref, g_ref, sc_ref, sh_ref, h_s)

    acc = _dot(h_s[...], w_ref[...].astype(BF16))
    cos = cos_ref[...]
    sin = sin_ref[...]
    for c in range(tn // LANES):
        chunk = _rope_chunk(acc[:, c * LANES:(c + 1) * LANES], cos, sin, half)
        o_ref[:, c, :, :] = chunk.reshape(bb, tl, LANES)


def _qkv(x, g, scale, shift, cos, sin, w, layer, *, half, tl):
    b, l, d = x.shape
    n = w.shape[2]
    nh = d // HEAD_DIM
    tn = 512
    hpt = tn // HEAD_DIM
    tpk = d // tn
    bb = _batch_block(b, l, tl)
    table = pl.BlockSpec((None, bb * tl, LANES), lambda i, j, k: (((k // tpk) % 3) // 2, j, 0))
    return pl.pallas_call(
        functools.partial(_qkv_kernel, half=half),
        grid=(b // bb, l // tl, n // tn),
        in_specs=[pl.BlockSpec((bb, tl, d), lambda i, j, k: (i, j, 0)),
                  pl.BlockSpec((1, d), lambda i, j, k: (0, 0)),
                  pl.BlockSpec((bb, 1, d), lambda i, j, k: (i, 0, 0)),
                  pl.BlockSpec((bb, 1, d), lambda i, j, k: (i, 0, 0)),
                  table, table,
                  pl.BlockSpec((None, d, tn), lambda i, j, k: (layer, 0, k))],
        out_specs=pl.BlockSpec((bb, None, hpt, tl, HEAD_DIM), lambda i, j, k: (i, k // tpk, k % tpk, j, 0)),
        out_shape=jax.ShapeDtypeStruct((b, n // d, nh, l, HEAD_DIM), F32),
        scratch_shapes=[pltpu.VMEM((bb * tl, d), BF16)],
        compiler_params=_params("arbitrary", "arbitrary", "arbitrary"),
        name="qkv",
    )(x, g.reshape(1, d), scale, shift, cos, sin, w)


def _proj_kernel(a_ref, w_ref, gate_ref, res_ref, o_ref, a_s):
    n = pl.program_id(2)
    bb, nh, tl, hd = a_ref.shape
    tn = w_ref.shape[1]

    @pl.when(n == 0)
    def _():
        def body(rows):
            for h in range(nh):
                a_s[rows, h * hd:(h + 1) * hd] = a_ref[:, h, rows, :].reshape(-1, hd).astype(BF16)

        _for_row_chunks(bb, tl, body)

    acc = _dot(a_s[...], w_ref[...].astype(BF16))
    o_ref[...] = res_ref[...] + gate_ref[...] * acc.reshape(bb, tl, tn)


def _proj(a, w, layer, gate, res, *, tl):
    b, l, d = res.shape
    nh = a.shape[1]
    tn = 512
    bb = _batch_block(b, l, tl)
    return pl.pallas_call(
        _proj_kernel,
        grid=(b // bb, l // tl, d // tn),
        in_specs=[pl.BlockSpec((bb, nh, tl, HEAD_DIM), lambda i, j, k: (i, 0, j, 0)),
                  pl.BlockSpec((None, d, tn), lambda i, j, k: (layer, 0, k)),
                  pl.BlockSpec((bb, 1, tn), lambda i, j, k: (i, 0, k)),
                  pl.BlockSpec((bb, tl, tn), lambda i, j, k: (i, j, k))],
        out_specs=pl.BlockSpec((bb, tl, tn), lambda i, j, k: (i, j, k)),
        out_shape=jax.ShapeDtypeStruct((b, l, d), F32),
        scratch_shapes=[pltpu.VMEM((bb * tl, d), BF16)],
        compiler_params=_params("arbitrary", "arbitrary", "arbitrary"),
        name="proj",
    )(a, w, gate, res)


def _mlp_kernel(x_ref, g_ref, sc_ref, sh_ref, gate_ref, wu_ref, wd_ref, o_ref, h_s):
    f = pl.program_id(2)
    bb, tl, d = x_ref.shape
    nchunk = 512

    @pl.when(f == 0)
    def _():
        _store_norm_mod(x_ref, g_ref, sc_ref, sh_ref, h_s)
        o_ref[...] = jnp.zeros_like(o_ref)

    a = jnp.maximum(_dot(h_s[...], wu_ref[...].astype(BF16)), 0.0)
    a = (a * a).astype(BF16)
    for c in range(d // nchunk):
        sl = slice(c * nchunk, (c + 1) * nchunk)
        o_ref[:, :, sl] += _dot(a, wd_ref[:, sl].astype(BF16)).reshape(bb, tl, nchunk)

    @pl.when(f == pl.num_programs(2) - 1)
    def _():
        o_ref[...] = x_ref[...] + gate_ref[...] * o_ref[...]


def _mlp(x, g, scale, shift, gate, w_up, w_down, layer, *, tl):
    b, l, d = x.shape
    ff = w_up.shape[2]
    bb = _batch_block(b, l, tl)
    big = bb * tl >= 1024
    tf = 512
    vec = pl.BlockSpec((bb, 1, d), lambda i, j, k: (i, 0, 0))
    x_mode = dict(pipeline_mode=pl.Buffered(1)) if big else {}
    return pl.pallas_call(
        _mlp_kernel,
        grid=(b // bb, l // tl, ff // tf),
        in_specs=[pl.BlockSpec((bb, tl, d), lambda i, j, k: (i, j, 0), **x_mode),
                  pl.BlockSpec((1, d), lambda i, j, k: (0, 0)),
                  vec, vec, vec,
                  pl.BlockSpec((None, d, tf), lambda i, j, k: (layer, 0, k)),
                  pl.BlockSpec((None, tf, d), lambda i, j, k: (layer, k, 0))],
        out_specs=pl.BlockSpec((bb, tl, d), lambda i, j, k: (i, j, 0), **x_mode),
        out_shape=jax.ShapeDtypeStruct((b, l, d), F32),
        scratch_shapes=[pltpu.VMEM((bb * tl, d), BF16)],
        compiler_params=_params("arbitrary", "arbitrary", "arbitrary"),
        name="mlp",
    )(x, g.reshape(1, d), scale, shift, gate, w_up, w_down)


def _final_norm_kernel(x_ref, g_ref, o_ref):
    x = x_ref[...]
    o_ref[...] = x * lax.rsqrt(jnp.mean(x * x, axis=-1, keepdims=True) + NORM_EPS) * g_ref[...]


def _final_norm(x, g):
    b, l, d = x.shape
    tl = min(l, 512)
    return pl.pallas_call(
        _final_norm_kernel,
        grid=(b, l // tl),
        in_specs=[pl.BlockSpec((1, tl, d), lambda i, j: (i, j, 0)),
                  pl.BlockSpec((1, d), lambda i, j: (0, 0))],
        out_specs=pl.BlockSpec((1, tl, d), lambda i, j: (i, j, 0)),
        out_shape=jax.ShapeDtypeStruct((b, l, d), F32),
        compiler_params=_params("arbitrary", "arbitrary"),
        name="final_norm",
    )(x, g.reshape(1, d))


def _online_update(s, v, m_s, l_s, acc_s, exp=jnp.exp):
    m_prev = m_s[...]
    m_new = jnp.maximum(m_prev, jnp.max(s, axis=1, keepdims=True))
    alpha = exp(m_prev - m_new)
    p = exp(s - m_new)
    l_s[...] = alpha * l_s[...] + jnp.sum(p, axis=1, keepdims=True)
    acc_s[...] = alpha * acc_s[...] + _dot(p.astype(BF16), v)
    m_s[...] = m_new


def _lane_column(x, j):
    lane = lax.broadcasted_iota(jnp.int32, x.shape, 1)
    return jnp.sum(jnp.where(lane == j, x, 0.0), axis=1, keepdims=True)


def _topk_bias(gate, n_valid, k):
    lane = lax.broadcasted_iota(jnp.int32, gate.shape, 1).astype(F32)
    g = jnp.where(lane < n_valid, gate, NEG)
    bias = jnp.full(gate.shape, NEG, F32)
    for _ in range(k):
        mx = jnp.max(g, axis=1, keepdims=True)
        first = jnp.min(jnp.where(g == mx, lane, float(gate.shape[1])), axis=1, keepdims=True)
        pick = (lane == first) & (mx > 0.5 * NEG)
        bias = jnp.where(pick, 0.0, bias)
        g = jnp.where(lane == first, NEG, g)
    return bias


def _diff_lambda(lam, lam_init):
    return (jnp.exp(jnp.sum(lam[0:1] * lam[1:2], axis=1, keepdims=True))
            - jnp.exp(jnp.sum(lam[2:3] * lam[3:4], axis=1, keepdims=True)) + lam_init)


def _sub_norm(o, gsub, lam_init):
    return o * lax.rsqrt(jnp.mean(o * o, axis=-1, keepdims=True) + SUBLN_EPS) * gsub * (1.0 - lam_init)


def _win_prompt_kernel(*refs, nw):
    in_refs, (o_ref, kcat_s, vcat_s, og_s, lse_s) = refs[:5 * N_GROUPS], refs[5 * N_GROUPS:]
    t = pl.program_id(2)
    tile = o_ref.shape[0]
    scale = HEAD_DIM ** -0.5 * LOG2E
    r = lax.broadcasted_iota(jnp.int32, (nw, 2 * nw), 0)
    c = lax.broadcasted_iota(jnp.int32, (nw, 2 * nw), 1)
    dist = nw + r - c
    band = (dist >= 0) & (dist <= nw)

    for g, (window, dil) in enumerate(DIL_GROUPS):
        q_ref, kc_ref, kp_ref, vc_ref, vp_ref = in_refs[5 * g:5 * g + 5]
        kcat_s[0:window, :] = kp_ref[...]
        kcat_s[window:window + tile, :] = kc_ref[...]
        vcat_s[0:window, :] = vp_ref[...]
        vcat_s[window:window + tile, :] = vc_ref[...]

        def blocks(ug, carry, g=g, dil=dil, window=window, q_ref=q_ref):
            work = []
            for i in range(WIN_UNROLL):
                u = ug * WIN_UNROLL + i
                start = u % dil + (u // dil) * window
                if dil == 1:
                    rows_q = pl.ds(start, nw)
                    rows_k = pl.ds(start, 2 * nw)
                else:
                    rows_q = pl.ds(start, nw, stride=dil)
                    rows_k = pl.ds(start, 2 * nw, stride=dil)
                q = (q_ref[rows_q, :] * scale).astype(BF16)
                k = kcat_s[rows_k, :].astype(BF16)
                first_key = jnp.where((t > 0) | (u >= dil), 0, nw)
                s = _dot_nt(q, k) + jnp.where(band & (c >= first_key), 0.0, NEG)
                work.append((rows_q, rows_k, s))
            for rows_q, rows_k, s in work:
                v = vcat_s[rows_k, :].astype(BF16)
                m = jnp.max(s, axis=1, keepdims=True)
                p = jnp.exp2(s - m)
                l = jnp.sum(p, axis=1, keepdims=True)
                og_s[g, rows_q, :] = _dot(p.astype(BF16), v) * (1.0 / l)
                lse_s[g, rows_q, :] = jnp.broadcast_to(m + jnp.log2(l), (nw, LANES))
            return carry

        lax.fori_loop(0, tile // nw // WIN_UNROLL, blocks, 0)

    def merge(rows):
        lses = [lse_s[g, rows, :] for g in range(N_GROUPS)]
        m = jnp.maximum(jnp.maximum(lses[0], lses[1]), lses[2])
        es = [jnp.exp2(v - m) for v in lses]
        inv = 1.0 / (es[0] + es[1] + es[2])
        o_ref[rows, :] = sum(es[g] * og_s[g, rows, :] for g in range(N_GROUPS)) * inv

    _for_row_chunks(1, tile, merge)


def _win_prompt(qkv):
    b, _, nh, l, hd = qkv.shape
    tile = WIN_TILE
    nw = DIL_GROUPS[0][0] // DIL_GROUPS[0][1]
    specs = []
    for g, (window, dil) in enumerate(DIL_GROUPS):
        assert window // dil == nw and tile % window == 0

        def cur(t, g=g):
            return pl.BlockSpec((None, None, None, tile, hd), lambda bi, h, i: (bi, 3 * g + t, h, i, 0))

        def prev(t, g=g, per=tile // window, window=window):
            return pl.BlockSpec((None, None, None, window, hd),
                                lambda bi, h, i: (bi, 3 * g + t, h, jnp.maximum(i * per - 1, 0), 0))

        specs += [cur(0), cur(1), prev(1), cur(2), prev(2)]
    max_w = DIL_GROUPS[-1][0]
    return pl.pallas_call(
        functools.partial(_win_prompt_kernel, nw=nw),
        grid=(b, nh, l // tile),
        in_specs=specs,
        out_specs=pl.BlockSpec((None, None, tile, hd), lambda bi, h, i: (bi, h, i, 0)),
        out_shape=jax.ShapeDtypeStruct((b, nh, l, hd), F32),
        scratch_shapes=[pltpu.VMEM((max_w + tile, hd), F32), pltpu.VMEM((max_w + tile, hd), F32),
                        pltpu.VMEM((N_GROUPS, tile, hd), F32), pltpu.VMEM((N_GROUPS, tile, LANES), F32)],
        compiler_params=_params("arbitrary", "arbitrary", "arbitrary"),
        name="win_prompt",
    )(*([qkv] * (5 * N_GROUPS)))


def _flash_prompt_kernel(*refs, diff, lam_init, tq):
    if diff:
        lam_ref, gsub_ref, q_ref, k_ref, v_ref, o_ref, qs_s, m_s, l_s, acc_s = refs
    else:
        km_ref, q_ref, k_ref, v_ref, o_ref, qs_s, m_s, l_s, acc_s = refs
    qi = pl.program_id(2)
    rows = qs_s.shape[0]
    lane = lax.broadcasted_iota(jnp.int32, (tq, HEAD_DIM), 1)
    q = q_ref[...]
    if diff:
        qsc = q * (DIFF_HALF ** -0.5 * LOG2E)
        qs_s[...] = jnp.concatenate([jnp.where(lane < DIFF_HALF, qsc, 0.0),
                                     jnp.where(lane >= DIFF_HALF, qsc, 0.0)], axis=0).astype(BF16)
    else:
        qs_s[...] = (q * (HEAD_DIM ** -0.5 * LOG2E)).astype(BF16)
        own = (qi * tq + lax.broadcasted_iota(jnp.int32, (tq, 1), 0)) // MOBA_BLOCK
        sel_bias = _topk_bias(_dot_nt_3pass(q, km_ref[...]), own.astype(F32), MOBA_TOPK)
    m_s[...] = jnp.full(m_s.shape, NEG, F32)
    l_s[...] = jnp.zeros(l_s.shape, F32)
    acc_s[...] = jnp.zeros(acc_s.shape, F32)

    chunk = FLASH_CHUNK
    diag = qi // (chunk // tq)

    def selection_bias(start, n_keys):
        cols = [jnp.broadcast_to(_lane_column(sel_bias, start // MOBA_BLOCK + t), (tq, MOBA_BLOCK))
                for t in range(n_keys // MOBA_BLOCK)]
        return jnp.concatenate(cols, axis=1)

    def consume(start, n_keys, bias):
        rows_k = pl.ds(pl.multiple_of(start, n_keys), n_keys)
        k = k_ref[rows_k, :].astype(BF16)
        v = v_ref[rows_k, :].astype(BF16)
        sub = FLASH_ROWS
        scores = [_dot_nt(qs_s[r0:r0 + sub, :], k) for r0 in range(0, rows, sub)]
        for r0, s in zip(range(0, rows, sub), scores):
            rs = slice(r0, r0 + sub)
            if bias is not None:
                s = s + bias[r0 % tq:r0 % tq + sub]
            _online_update(s, v, m_s.at[rs, :], l_s.at[rs, :], acc_s.at[rs, :], exp=jnp.exp2)

    def past_chunk(ci, carry):
        consume(ci * chunk, chunk, None if diff else selection_bias(ci * chunk, chunk))
        return carry

    lax.fori_loop(0, diag, past_chunk, 0)

    def diagonal_part(start, n_keys):
        q_pos = qi * tq + lax.broadcasted_iota(jnp.int32, (tq, n_keys), 0)
        k_pos = start + lax.broadcasted_iota(jnp.int32, (tq, n_keys), 1)
        bias = jnp.where(k_pos <= q_pos, 0.0, NEG)
        if not diff:
            bias = jnp.where(k_pos // MOBA_BLOCK >= q_pos // MOBA_BLOCK, bias, selection_bias(start, n_keys))
        consume(start, n_keys, bias)

    diagonal_part(diag * chunk, chunk)

    o = acc_s[...] * (1.0 / l_s[...])
    if diff:
        o = o[:tq] - _diff_lambda(lam_ref[...], lam_init) * o[tq:]
        o = _sub_norm(o, gsub_ref[...], lam_init)
    o_ref[...] = o


def _flash_prompt(qkv, *, diff, lam=None, gsub=None, lam_init=0.0, kmean=None):
    b, _, nh, l, hd = qkv.shape
    tq = MOBA_BLOCK if diff else 2 * MOBA_BLOCK
    assert l % FLASH_CHUNK == 0 and FLASH_CHUNK % tq == 0
    rows = 2 * tq if diff else tq
    qkv_specs = [pl.BlockSpec((None, None, None, tq, hd), lambda bi, h, i: (bi, 0, h, i, 0)),
                 pl.BlockSpec((None, None, None, l, hd), lambda bi, h, i: (bi, 1, h, 0, 0)),
                 pl.BlockSpec((None, None, None, l, hd), lambda bi, h, i: (bi, 2, h, 0, 0))]
    if diff:
        extra = [lam, gsub.reshape(1, hd)]
        extra_specs = [pl.BlockSpec(lam.shape, lambda bi, h, i: (0, 0)),
                       pl.BlockSpec((1, hd), lambda bi, h, i: (0, 0))]
    else:
        extra = [kmean]
        extra_specs = [pl.BlockSpec((None, None, kmean.shape[2], hd), lambda bi, h, i: (bi, h, 0, 0))]
    return pl.pallas_call(
        functools.partial(_flash_prompt_kernel, diff=diff, lam_init=lam_init, tq=tq),
        grid=(b, nh, l // tq),
        in_specs=extra_specs + qkv_specs,
        out_specs=pl.BlockSpec((None, None, tq, hd), lambda bi, h, i: (bi, h, i, 0)),
        out_shape=jax.ShapeDtypeStruct((b, nh, l, hd), F32),
        scratch_shapes=[pltpu.VMEM((rows, hd), BF16), pltpu.VMEM((rows, 1), F32),
                        pltpu.VMEM((rows, 1), F32), pltpu.VMEM((rows, hd), F32)],
        compiler_params=_params("arbitrary", "arbitrary", "arbitrary"),
        name="flash_diff" if diff else "flash_moba",
    )(*extra, qkv, qkv, qkv)


def _block_mean_kernel(k_ref, o_ref):
    l, hd = k_ref.shape
    o_ref[...] = jnp.sum(k_ref[...].reshape(l // MOBA_BLOCK, MOBA_BLOCK, hd), axis=1) * (1.0 / MOBA_BLOCK)


def _block_mean_prompt(qkv):
    b, _, nh, l, hd = qkv.shape
    nb = l // MOBA_BLOCK
    return pl.pallas_call(
        _block_mean_kernel,
        grid=(b, nh),
        in_specs=[pl.BlockSpec((None, None, None, l, hd), lambda bi, h: (bi, 1, h, 0, 0))],
        out_specs=pl.BlockSpec((None, None, nb, hd), lambda bi, h: (bi, h, 0, 0)),
        out_shape=jax.ShapeDtypeStruct((b, nh, nb, hd), F32),
        compiler_params=_params("arbitrary", "arbitrary"),
        name="block_mean",
    )(qkv)


def _decode_kernel(*refs, mode, n_pg, paged, lam_init):
    refs = list(refs)
    if paged:
        refs.pop(0)
    q_ref = refs.pop(0)
    pg_refs = [refs.pop(0) for _ in range(n_pg)]
    kn_ref, vn_ref = refs.pop(0), refs.pop(0)
    bias_ref, bnew_ref = refs.pop(0), refs.pop(0)
    km_ref = refs.pop(0) if mode == "moba" else None
    if mode == "diff":
        lam_ref, gsub_ref = refs.pop(0), refs.pop(0)
    o_ref = refs.pop(0)
    lse_ref = refs.pop(0) if mode == "win" else None
    q_s, kf_s, vf_s, m_s, l_s, acc_s = refs[:6]
    sel_s = refs[6] if mode == "moba" else None

    s_id = pl.program_id(1)
    nh, n, hd = q_ref.shape
    rows = q_s.shape[0]
    half = rows // 2
    hpb = nh // 2
    tk = pg_refs[0].shape[0]
    scale = (DIFF_HALF if mode == "diff" else HEAD_DIM) ** -0.5

    @pl.when(s_id == 0)
    def _():
        lane = lax.broadcasted_iota(jnp.int32, (n, hd), 1)
        pieces = []
        for h in range(nh):
            qh = q_ref[h] * scale
            if mode == "diff":
                pieces += [jnp.where(lane < DIFF_HALF, qh, 0.0), jnp.where(lane >= DIFF_HALF, qh, 0.0)]
            else:
                pieces.append(qh)
        q_s[...] = jnp.concatenate(pieces, axis=0).astype(BF16)
        m_s[...] = jnp.full(m_s.shape, NEG, F32)
        l_s[...] = jnp.zeros(l_s.shape, F32)
        acc_s[...] = jnp.zeros(acc_s.shape, F32)
        if mode == "moba":
            n_blk = km_ref.shape[0] // nh
            for h in range(nh):
                gate = _dot_nt_3pass(q_ref[h], km_ref[pl.ds(h, n_blk, stride=nh), :])
                sel_s[h * n:(h + 1) * n, :] = _topk_bias(gate, n_blk, MOBA_TOPK)

    for hb in range(2):
        hs = slice(hb * hpb, (hb + 1) * hpb)
        rs = slice(hb * half, (hb + 1) * half)
        for r in range(n_pg):
            dst = slice(r * tk * hpb, (r + 1) * tk * hpb)
            kf_s[dst, :] = pg_refs[r][:, 0, hs, :].reshape(tk * hpb, hd).astype(BF16)
            vf_s[dst, :] = pg_refs[r][:, 1, hs, :].reshape(tk * hpb, hd).astype(BF16)
        s = _dot_nt(q_s[rs, :], kf_s[...]) + bias_ref[...]
        if mode == "moba":
            pages_per_blk = MOBA_BLOCK // tk
            cols = []
            for r in range(n_pg // pages_per_blk):
                col = _lane_column(sel_s[rs, :], s_id * (n_pg // pages_per_blk) + r)
                cols.append(jnp.broadcast_to(col, (half, MOBA_BLOCK * hpb)))
            s = s + jnp.concatenate(cols, axis=1)
        _online_update(s, vf_s[...], m_s.at[rs, :], l_s.at[rs, :], acc_s.at[rs, :])

    @pl.when(s_id == pl.num_programs(1) - 1)
    def _():
        pad = jnp.zeros((LANES - hpb * n, hd), F32)
        for hb in range(2):
            hs = slice(hb * hpb, (hb + 1) * hpb)
            rs = slice(hb * half, (hb + 1) * half)
            kn = jnp.concatenate([kn_ref[hs].reshape(hpb * n, hd), pad], axis=0).astype(BF16)
            vn = jnp.concatenate([vn_ref[hs].reshape(hpb * n, hd), pad], axis=0).astype(BF16)
            _online_update(_dot_nt(q_s[rs, :], kn) + bnew_ref[...], vn,
                           m_s.at[rs, :], l_s.at[rs, :], acc_s.at[rs, :])
        lsum = l_s[...]
        o = acc_s[...] * (1.0 / lsum)
        if mode == "diff":
            lam_full = _diff_lambda(lam_ref[...], lam_init)
            for h in range(nh):
                oh = o[2 * h * n:(2 * h + 1) * n] - lam_full * o[(2 * h + 1) * n:(2 * h + 2) * n]
                o_ref[h] = _sub_norm(oh, gsub_ref[...], lam_init)
        else:
            o_ref[...] = o.reshape(nh, n, hd)
        if mode == "win":
            lse_ref[...] = jnp.broadcast_to(m_s[...] + jnp.log(lsum), (rows, LANES)).reshape(nh, n, LANES)


def _head_match_bias(half_rows, rows_per_head, n_keys, hpb):
    r = np.arange(half_rows)[:, None] // rows_per_head
    c = np.arange(n_keys * hpb)[None, :] % hpb
    return np.where(r == c, 0.0, NEG)


def _decode_attn(qkv, gt, cache, layer, *, mode, page_table=None, dil=1, window=None,
                 kmean=None, lam=None, gsub=None, lam_init=0.0):
    b, _, nh, n, hd = qkv.shape
    hpb = nh // 2
    comps = 2 if mode == "diff" else 1
    rows = nh * comps * n
    half = rows // 2
    paged = page_table is not None
    tk = PAGE_SIZE
    total_pages = page_table.shape[1] if paged else cache.shape[2] // tk
    n_pg = min(PAGES_PER_STEP, total_pages)
    steps = total_pages // n_pg
    blk = (None, None, tk, 2, nh, hd)
    if paged:
        pg_specs = [pl.BlockSpec(blk, functools.partial(
            lambda bi, s, pt, r: (layer, pt[bi, s * n_pg + r], 0, 0, 0, 0), r=r)) for r in range(n_pg)]
    else:
        pg_specs = [pl.BlockSpec(blk, functools.partial(
            lambda bi, s, r: (layer, bi, s * n_pg + r, 0, 0, 0), r=r)) for r in range(n_pg)]

    def fixed(shape, idx):
        if paged:
            return pl.BlockSpec(shape, lambda bi, s, pt: idx(bi, s))
        return pl.BlockSpec(shape, lambda bi, s: idx(bi, s))

    qi = np.arange(half)[:, None] % n
    bias = _head_match_bias(half, comps * n, total_pages * tk if mode == "win" else n_pg * tk, hpb)
    cn = np.arange(LANES)[None, :]
    new_ok = (cn < hpb * n) & (cn // n == np.arange(half)[:, None] // (comps * n)) & (cn % n <= qi)
    if mode == "win":
        n_buf = total_pages * tk
        dist = n_buf + qi - np.arange(n_buf * hpb)[None, :] // hpb
        bias = np.where((dist % dil == 0) & (dist <= window), bias, NEG)
        new_ok &= (qi - cn % n) % dil == 0
    bias = jnp.asarray(bias, F32)
    bias_new = jnp.asarray(np.where(new_ok, 0.0, NEG), F32)

    head_blk = (None, None, nh, n, hd)
    args = [qkv] + [cache] * n_pg + [qkv, qkv, bias, bias_new]
    specs = [fixed(head_blk, lambda bi, s: (bi, gt, 0, 0, 0))] + pg_specs + [
        fixed(head_blk, lambda bi, s: (bi, gt + 1, 0, 0, 0)),
        fixed(head_blk, lambda bi, s: (bi, gt + 2, 0, 0, 0)),
        fixed((half, n_pg * tk * hpb), (lambda bi, s: (0, s)) if mode == "win" else (lambda bi, s: (0, 0))),
        fixed((half, LANES), lambda bi, s: (0, 0))]
    if mode == "moba":
        n_blk = kmean.shape[1]
        args.append(kmean.reshape(b, n_blk * nh, hd))
        specs.append(fixed((None, n_blk * nh, hd), lambda bi, s: (bi, 0, 0)))
    if mode == "diff":
        args += [lam, gsub.reshape(1, hd)]
        specs += [fixed(lam.shape, lambda bi, s: (0, 0)), fixed((1, hd), lambda bi, s: (0, 0))]
    scratch = [pltpu.VMEM((rows, hd), BF16),
               pltpu.VMEM((n_pg * tk * hpb, hd), BF16), pltpu.VMEM((n_pg * tk * hpb, hd), BF16),
               pltpu.VMEM((rows, 1), F32), pltpu.VMEM((rows, 1), F32), pltpu.VMEM((rows, hd), F32)]
    if mode == "moba":
        scratch.append(pltpu.VMEM((rows, n_blk), F32))
    out_blk = (None, nh, n, hd)
    out_shape = [jax.ShapeDtypeStruct((b, nh, n, hd), F32)]
    out_specs = [fixed(out_blk, lambda bi, s: (bi, 0, 0, 0))]
    if mode == "win":
        out_shape.append(jax.ShapeDtypeStruct((b, nh, n, LANES), F32))
        out_specs.append(fixed(out_blk, lambda bi, s: (bi, 0, 0, 0)))
    kern = functools.partial(_decode_kernel, mode=mode, n_pg=n_pg, paged=paged, lam_init=lam_init)
    name = "decode_" + mode
    if paged:
        grid_spec = pltpu.PrefetchScalarGridSpec(num_scalar_prefetch=1, grid=(b, steps), in_specs=specs,
                                                 out_specs=out_specs, scratch_shapes=scratch)
        out = pl.pallas_call(kern, grid_spec=grid_spec, out_shape=out_shape, name=name,
                             compiler_params=_params("arbitrary", "arbitrary"))(page_table, *args)
    else:
        out = pl.pallas_call(kern, grid=(b, steps), in_specs=specs, out_specs=out_specs, out_shape=out_shape,
                             scratch_shapes=scratch, name=name,
                             compiler_params=_params("arbitrary", "arbitrary"))(*args)
    return out if mode == "win" else out[0]


def _merge_groups_kernel(*refs):
    o_refs, lse_refs, out_ref = refs[:N_GROUPS], refs[N_GROUPS:2 * N_GROUPS], refs[2 * N_GROUPS]
    lses = [r[...] for r in lse_refs]
    m = jnp.maximum(jnp.maximum(lses[0], lses[1]), lses[2])
    es = [jnp.exp(v - m) for v in lses]
    inv = 1.0 / (es[0] + es[1] + es[2])
    out_ref[...] = sum(es[g] * o_refs[g][...] for g in range(N_GROUPS)) * inv


def _merge_groups(outs, lses):
    shape = outs[0].shape
    spec = pl.BlockSpec(shape, lambda i: (0,) * len(shape))
    return pl.pallas_call(
        _merge_groups_kernel, grid=(1,), in_specs=[spec] * (2 * N_GROUPS), out_specs=spec,
        out_shape=jax.ShapeDtypeStruct(shape, F32), compiler_params=_params("arbitrary"),
        name="merge_groups",
    )(*outs, *lses)


def _paged_mean_kernel(pt_ref, *refs, n_pages, pages_per_blk):
    o_ref = refs[n_pages]
    for j in range(n_pages // pages_per_blk):
        tot = sum(jnp.sum(refs[j * pages_per_blk + r][...], axis=0) for r in range(pages_per_blk))
        o_ref[j] = tot * (1.0 / MOBA_BLOCK)


def _paged_block_mean(cache, layer, page_table):
    b, total_pages = page_table.shape
    tk, _, nh, hd = cache.shape[2:]
    pages_per_blk = MOBA_BLOCK // tk
    n_pages = 8
    steps = total_pages // n_pages
    specs = [pl.BlockSpec((None, None, tk, None, nh, hd), functools.partial(
        lambda bi, s, pt, r: (layer, pt[bi, s * n_pages + r], 0, 0, 0, 0), r=r)) for r in range(n_pages)]
    return pl.pallas_call(
        functools.partial(_paged_mean_kernel, n_pages=n_pages, pages_per_blk=pages_per_blk),
        grid_spec=pltpu.PrefetchScalarGridSpec(
            num_scalar_prefetch=1, grid=(b, steps), in_specs=specs,
            out_specs=pl.BlockSpec((None, n_pages // pages_per_blk, nh, hd), lambda bi, s, pt: (bi, s, 0, 0))),
        out_shape=jax.ShapeDtypeStruct((b, total_pages // pages_per_blk, nh, hd), F32),
        compiler_params=_params("arbitrary", "arbitrary"),
        name="paged_block_mean",
    )(page_table, *([cache] * n_pages))


def _shift_window_kernel(cur_ref, nxt_ref, new_ref, o_ref):
    t = cur_ref.shape[0]
    n = new_ref.shape[0]
    o_ref[0:t - n] = cur_ref[n:t]
    last = pl.program_id(2) == pl.num_programs(2) - 1

    @pl.when(last)
    def _():
        o_ref[t - n:t] = new_ref[...]

    @pl.when(jnp.logical_not(last))
    def _():
        o_ref[t - n:t] = nxt_ref[...]


def _shift_window(cache, new):
    layers, b, w = cache.shape[:3]
    n = new.shape[2]
    row = cache.shape[3:]
    t = min(w, 512)
    steps = w // t
    last_head = w // n - 1
    return pl.pallas_call(
        _shift_window_kernel,
        grid=(layers, b, steps),
        in_specs=[pl.BlockSpec((None, None, t) + row, lambda j, bi, i: (j, bi, i, 0, 0, 0)),
                  pl.BlockSpec((None, None, n) + row,
                               lambda j, bi, i: (j, bi, jnp.minimum((i + 1) * (t // n), last_head), 0, 0, 0)),
                  pl.BlockSpec((None, None, n) + row, lambda j, bi, i: (j, bi, 0, 0, 0, 0))],
        out_specs=pl.BlockSpec((None, None, t) + row, lambda j, bi, i: (j, bi, i, 0, 0, 0)),
        out_shape=jax.ShapeDtypeStruct(cache.shape, cache.dtype),
        compiler_params=_params("arbitrary", "arbitrary", "arbitrary"),
        name="shift_window",
    )(cache, cache, new)


def _rows_kernel(a_ref, b_ref, o_ref):
    layer = pl.program_id(0)
    nh = a_ref.shape[0]

    @pl.when(layer == 0)
    def _():
        for h in range(nh):
            o_ref[:, h, :] = a_ref[h]

    @pl.when(layer == 1)
    def _():
        for h in range(nh):
            o_ref[:, h, :] = b_ref[h]


def _rows_of_two(kv_a, kv_b, first, n_rows):
    b, _, nh, seq, hd = kv_a.shape
    l = n_rows
    tl = min(512, l)
    nt = l // tl
    assert (seq - l) % tl == 0
    skip = (seq - l) // tl

    def src(which):
        def index(layer, bi, i, kv):
            live = layer == which
            return (jnp.where(live, bi, (1 - which) * (b - 1)), first + jnp.where(live, kv, (1 - which)), 0,
                    skip + jnp.where(live, i, (1 - which) * (nt - 1)), 0)
        return pl.BlockSpec((None, None, nh, tl, hd), index)

    return pl.pallas_call(
        _rows_kernel,
        grid=(2, b, nt, 2),
        in_specs=[src(0), src(1)],
        out_specs=pl.BlockSpec((None, None, tl, None, nh, hd), lambda layer, bi, i, kv: (layer, bi, i, kv, 0, 0)),
        out_shape=jax.ShapeDtypeStruct((2, b, l, 2, nh, hd), F32),
        compiler_params=_params("arbitrary", "arbitrary", "arbitrary", "arbitrary"),
        name="rows_of",
    )(kv_a, kv_b)


def _rope_tables(pos, half, reps=1):
    inv = ROPE_THETA ** (-np.arange(0, 2 * half, 2, dtype=np.float64) / (2 * half))
    ang = np.asarray(pos, np.float64)[:, None] * inv[None, :]
    cos = np.concatenate([np.cos(ang), np.cos(ang)], axis=1)
    sin = np.concatenate([-np.sin(ang), np.sin(ang)], axis=1)
    tile = LANES // (2 * half)
    cos = np.tile(np.tile(cos, (1, tile)), (reps, 1))
    sin = np.tile(np.tile(sin, (1, tile)), (reps, 1))
    return (jnp.asarray(np.stack([cos, np.ones_like(cos)]), F32),
            jnp.asarray(np.stack([sin, np.zeros_like(sin)]), F32))


def _rows_of(kv_layers):
    return jnp.transpose(jnp.stack(kv_layers), (0, 1, 4, 2, 3, 5))


def kernel(x_prompt, x_sample, c_prompt, c_sample, cache_win1_kv, cache_win2_kv, cache_win3_kv, cache_kv,
           page_table, norm1_g, norm2_g, w_mod, b_mod, w_qkv_a, w_o_a, w_qkv_b, w_o_b, lambda_b, subln_g_b,
           w_qkv_c, w_o_c, w_up, w_down, final_g):
    bp, seq, d = x_prompt.shape
    bs, n_new, _ = x_sample.shape
    depth = w_mod.shape[0]
    past_len = page_table.shape[1] * PAGE_SIZE
    assert past_len % MOBA_BLOCK == 0 and seq % WIN_TILE == 0 and n_new == SUBLANES
    assert cache_kv.shape[2] == PAGE_SIZE
    pos_p = np.arange(seq)
    pos_s = past_len + np.arange(n_new)
    tl_p = 1024

    rope_p = {h: _rope_tables(pos_p, h) for h in (HEAD_DIM // 2, DIFF_HALF // 2)}
    rope_s = {h: _rope_tables(pos_s, h, reps=bs) for h in (HEAD_DIM // 2, DIFF_HALF // 2)}

    rows_c = 2 * SUBLANES
    c_all = jnp.concatenate([c_prompt, c_sample, jnp.zeros((rows_c - bp - bs, d), F32)], axis=0)
    mods = _modulation(c_all, w_mod, b_mod)
    win_caches = (cache_win1_kv, cache_win2_kv, cache_win3_kv)

    xp, xs = x_prompt, x_sample
    win_p = [[] for _ in range(N_GROUPS)]
    win_s = [[] for _ in range(N_GROUPS)]
    kv_p, kv_s = [], []
    for i in range(depth):
        kind = i % N_MIXERS
        j = i // N_MIXERS
        mp = [mods[i, :bp, t * d:(t + 1) * d][:, None, :] for t in range(6)]
        ms = [mods[i, bp:bp + bs, t * d:(t + 1) * d][:, None, :] for t in range(6)]
        if kind == 0:
            half = HEAD_DIM // 2
            qkv_p = _qkv(xp, norm1_g[i], mp[1], mp[0], *rope_p[half], w_qkv_a, j, half=half, tl=tl_p)
            qkv_s = _qkv(xs, norm1_g[i], ms[1], ms[0], *rope_s[half], w_qkv_a, j, half=half, tl=n_new)
            o_p = _win_prompt(qkv_p)
            outs_s, lses_s = [], []
            for g, (window, dil) in enumerate(DIL_GROUPS):
                win_p[g].append(qkv_p)
                o_g, lse_g = _decode_attn(qkv_s, 3 * g, win_caches[g], j, mode="win", dil=dil, window=window)
                outs_s.append(o_g)
                lses_s.append(lse_g)
                win_s[g].append(qkv_s[:, 3 * g + 1:3 * g + 3])
            o_s = _merge_groups(outs_s, lses_s)
            w_o = w_o_a
        else:
            slot = i - (i + 2) // N_MIXERS
            if kind == 1:
                half = DIFF_HALF // 2
                w_qkv, w_o = w_qkv_b, w_o_b
            else:
                half = HEAD_DIM // 2
                w_qkv, w_o = w_qkv_c, w_o_c
            qkv_p = _qkv(xp, norm1_g[i], mp[1], mp[0], *rope_p[half], w_qkv, j, half=half, tl=tl_p)
            qkv_s = _qkv(xs, norm1_g[i], ms[1], ms[0], *rope_s[half], w_qkv, j, half=half, tl=n_new)
            if kind == 1:
                lam_init = 0.8 - 0.6 * math.exp(-0.3 * i)
                o_p = _flash_prompt(qkv_p, diff=True, lam=lambda_b[j], gsub=subln_g_b[j], lam_init=lam_init)
                o_s = _decode_attn(qkv_s, 0, cache_kv, slot, mode="diff", page_table=page_table,
                                   lam=lambda_b[j], gsub=subln_g_b[j], lam_init=lam_init)
            else:
                o_p = _flash_prompt(qkv_p, diff=False, kmean=_block_mean_prompt(qkv_p))
                o_s = _decode_attn(qkv_s, 0, cache_kv, slot, mode="moba", page_table=page_table,
                                   kmean=_paged_block_mean(cache_kv, slot, page_table))
            kv_p.append(qkv_p)
            kv_s.append(qkv_s[:, 1:3])
        xp = _proj(o_p, w_o, j, mp[2], xp, tl=tl_p)
        xs = _proj(o_s, w_o, j, ms[2], xs, tl=n_new)
        xp = _mlp(xp, norm2_g[i], mp[4], mp[3], mp[5], w_up, w_down, i, tl=tl_p)
        xs = _mlp(xs, norm2_g[i], ms[4], ms[3], ms[5], w_up, w_down, i, tl=n_new)
    y_prompt = _final_norm(xp, final_g)
    y_sample = _final_norm(xs, final_g)
    new_win_s = [_shift_window(win_caches[g], _rows_of(win_s[g])) for g in range(N_GROUPS)]
    def prompt_rows(layers, first, keep):
        if len(layers) == 2:
            return _rows_of_two(layers[0], layers[1], first, keep)
        return _rows_of([t[:, first:first + 2, :, seq - keep:] for t in layers])

    new_win_p = [prompt_rows(win_p[g], 3 * g + 1, min(DIL_GROUPS[g][0], seq)) for g in range(N_GROUPS)]
    new_kv_p = prompt_rows(kv_p, 1, seq)
    return (y_prompt, y_sample,
            new_win_p[0], new_win_p[1], new_win_p[2], new_kv_p,
            new_win_s[0], new_win_s[1], new_win_s[2], _rows_of(kv_s))
```

```python
import functools
import math

import numpy as np
import jax
import jax.numpy as jnp
from jax import lax
from jax.experimental import pallas as pl
from jax.experimental.pallas import tpu as pltpu

F32 = jnp.float32
BF16 = jnp.bfloat16

HEAD_DIM = 128
DIL_GROUPS = ((128, 1), (512, 4), (2048, 16))
N_GROUPS = len(DIL_GROUPS)
DIFF_HALF = HEAD_DIM // 2
MOBA_BLOCK = 256
MOBA_TOPK = 3
PAGE_SIZE = 128
ROPE_THETA = 10000.0
NORM_EPS = 1e-6
SUBLN_EPS = 1e-5
N_MIXERS = 3

LANES = 128
SUBLANES = 8
NEG = -1e30
LOG2E = math.log2(math.e)
VMEM_LIMIT = 56 * 1024 * 1024
PAGES_PER_STEP = 4
ROW_CHUNK = 128
WIN_TILE = 2048
WIN_UNROLL = 4
FLASH_CHUNK = 2048
FLASH_ROWS = 128


def _params(*sem):
    return pltpu.CompilerParams(dimension_semantics=sem, vmem_limit_bytes=VMEM_LIMIT)


def _dot(a, b):
    return jnp.dot(a, b, preferred_element_type=F32)


def _dot_nt(a, b):
    return lax.dot_general(a, b, (((1,), (1,)), ((), ())), preferred_element_type=F32)


def _split_bf16(x):
    hi = x.astype(BF16)
    lo = (x - hi.astype(F32)).astype(BF16)
    return hi, lo


def _dot_nt_3pass(a, b):
    ah, al = _split_bf16(a)
    bh, bl = _split_bf16(b)
    return _dot_nt(ah, bh) + _dot_nt(ah, bl) + _dot_nt(al, bh)


def _norm_mod(x, g, scale, shift):
    y = x * lax.rsqrt(jnp.mean(x * x, axis=-1, keepdims=True) + NORM_EPS) * g
    return y * (1.0 + scale) + shift


def _for_row_chunks(bb, tl, body):
    if bb > 1 or tl <= ROW_CHUNK:
        body(slice(None))
        return

    def step(c, carry):
        body(pl.ds(pl.multiple_of(c * ROW_CHUNK, ROW_CHUNK), ROW_CHUNK))
        return carry

    lax.fori_loop(0, tl // ROW_CHUNK, step, 0)


def _store_norm_mod(x_ref, g_ref, sc_ref, sh_ref, h_s):
    bb, tl, d = x_ref.shape

    def body(rows):
        h = _norm_mod(x_ref[:, rows, :], g_ref[...], sc_ref[...], sh_ref[...])
        h_s[rows, :] = h.reshape(-1, d).astype(BF16)

    _for_row_chunks(bb, tl, body)


def _batch_block(b, l, tl):
    return b if tl == l and b * l <= 1024 else 1


def _mod_kernel(c_ref, w_ref, b_ref, o_ref):
    c = c_ref[...]
    a = (c / (1.0 + jnp.exp(-c))).astype(BF16)
    o_ref[...] = _dot(a, w_ref[...].astype(BF16)) + b_ref[...]


def _modulation(c_all, w_mod, b_mod):
    depth, d, n = w_mod.shape
    rows = c_all.shape[0]
    tn = 1024
    return pl.pallas_call(
        _mod_kernel,
        grid=(depth, n // tn),
        in_specs=[pl.BlockSpec((rows, d), lambda i, j: (0, 0)),
                  pl.BlockSpec((None, d, tn), lambda i, j: (i, 0, j)),
                  pl.BlockSpec((None, 1, tn), lambda i, j: (i, 0, j))],
        out_specs=pl.BlockSpec((None, rows, tn), lambda i, j: (i, 0, j)),
        out_shape=jax.ShapeDtypeStruct((depth, rows, n), F32),
        compiler_params=_params("arbitrary", "arbitrary"),
        name="modulation",
    )(c_all, w_mod, b_mod.reshape(depth, 1, n))


def _rope_chunk(x, cos, sin, half):
    if 2 * half == LANES:
        rot = pltpu.roll(x, half, 1)
    else:
        lane = lax.broadcasted_iota(jnp.int32, x.shape, 1)
        rot = jnp.where(lane % (2 * half) < half, pltpu.roll(x, LANES - half, 1), pltpu.roll(x, half, 1))
    return x * cos + rot * sin


def _qkv_kernel(x_ref, g_ref, sc_ref, sh_ref, cos_ref, sin_ref, w_ref, o_ref, h_s, *, half):
    n = pl.program_id(2)
    bb, tl, d = x_ref.shape
    tn = w_ref.shape[1]

    @pl.when(n == 0)
    def _():
        _store_norm_mod(x_ref, g_ref, sc_ref, sh_ref, h_s)

    acc = _dot(h_s[...], w_ref[...].astype(BF16))
    cos = cos_ref[...]
    sin = sin_ref[...]
    for c in range(tn // LANES):
        chunk = _rope_chunk(acc[:, c * LANES:(c + 1) * LANES], cos, sin, half)
        o_ref[:, c, :, :] = chunk.reshape(bb, tl, LANES)


def _qkv(x, g, scale, shift, cos, sin, w, layer, *, half, tl):
    b, l, d = x.shape
    n = w.shape[2]
    nh = d // HEAD_DIM
    tn = 512
    hpt = tn // HEAD_DIM
    tpk = d // tn
    bb = _batch_block(b, l, tl)
    table = pl.BlockSpec((None, bb * tl, LANES), lambda i, j, k: (((k // tpk) % 3) // 2, j, 0))
    return pl.pallas_call(
        functools.partial(_qkv_kernel, half=half),
        grid=(b // bb, l // tl, n // tn),
        in_specs=[pl.BlockSpec((bb, tl, d), lambda i, j, k: (i, j, 0)),
                  pl.BlockSpec((1, d), lambda i, j, k: (0, 0)),
                  pl.BlockSpec((bb, 1, d), lambda i, j, k: (i, 0, 0)),
                  pl.BlockSpec((bb, 1, d), lambda i, j, k: (i, 0, 0)),
                  table, table,
                  pl.BlockSpec((None, d, tn), lambda i, j, k: (layer, 0, k))],
        out_specs=pl.BlockSpec((bb, None, hpt, tl, HEAD_DIM), lambda i, j, k: (i, k // tpk, k % tpk, j, 0)),
        out_shape=jax.ShapeDtypeStruct((b, n // d, nh, l, HEAD_DIM), F32),
        scratch_shapes=[pltpu.VMEM((bb * tl, d), BF16)],
        compiler_params=_params("arbitrary", "arbitrary", "arbitrary"),
        name="qkv",
    )(x, g.reshape(1, d), scale, shift, cos, sin, w)


def _proj_kernel(a_ref, w_ref, gate_ref, res_ref, o_ref, a_s):
    n = pl.program_id(2)
    bb, nh, tl, hd = a_ref.shape
    tn = w_ref.shape[1]

    @pl.when(n == 0)
    def _():
        def body(rows):
            for h in range(nh):
                a_s[rows, h * hd:(h + 1) * hd] = a_ref[:, h, rows, :].reshape(-1, hd).astype(BF16)

        _for_row_chunks(bb, tl, body)

    acc = _dot(a_s[...], w_ref[...].astype(BF16))
    o_ref[...] = res_ref[...] + gate_ref[...] * acc.reshape(bb, tl, tn)


def _proj(a, w, layer, gate, res, *, tl):
    b, l, d = res.shape
    nh = a.shape[1]
    tn = 512
    bb = _batch_block(b, l, tl)
    return pl.pallas_call(
        _proj_kernel,
        grid=(b // bb, l // tl, d // tn),
        in_specs=[pl.BlockSpec((bb, nh, tl, HEAD_DIM), lambda i, j, k: (i, 0, j, 0)),
                  pl.BlockSpec((None, d, tn), lambda i, j, k: (layer, 0, k)),
                  pl.BlockSpec((bb, 1, tn), lambda i, j, k: (i, 0, k)),
                  pl.BlockSpec((bb, tl, tn), lambda i, j, k: (i, j, k))],
        out_specs=pl.BlockSpec((bb, tl, tn), lambda i, j, k: (i, j, k)),
        out_shape=jax.ShapeDtypeStruct((b, l, d), F32),
        scratch_shapes=[pltpu.VMEM((bb * tl, d), BF16)],
        compiler_params=_params("arbitrary", "arbitrary", "arbitrary"),
        name="proj",
    )(a, w, gate, res)


def _mlp_kernel(x_ref, g_ref, sc_ref, sh_ref, gate_ref, wu_ref, wd_ref, o_ref, h_s):
    f = pl.program_id(2)
    bb, tl, d = x_ref.shape
    nchunk = 512

    @pl.when(f == 0)
    def _():
        _store_norm_mod(x_ref, g_ref, sc_ref, sh_ref, h_s)
        o_ref[...] = jnp.zeros_like(o_ref)

    a = jnp.maximum(_dot(h_s[...], wu_ref[...].astype(BF16)), 0.0)
    a = (a * a).astype(BF16)
    for c in range(d // nchunk):
        sl = slice(c * nchunk, (c + 1) * nchunk)
        o_ref[:, :, sl] += _dot(a, wd_ref[:, sl].astype(BF16)).reshape(bb, tl, nchunk)

    @pl.when(f == pl.num_programs(2) - 1)
    def _():
        o_ref[...] = x_ref[...] + gate_ref[...] * o_ref[...]


def _mlp(x, g, scale, shift, gate, w_up, w_down, layer, *, tl):
    b, l, d = x.shape
    ff = w_up.shape[2]
    bb = _batch_block(b, l, tl)
    big = bb * tl >= 1024
    tf = 512
    vec = pl.BlockSpec((bb, 1, d), lambda i, j, k: (i, 0, 0))
    x_mode = dict(pipeline_mode=pl.Buffered(1)) if big else {}
    return pl.pallas_call(
        _mlp_kernel,
        grid=(b // bb, l // tl, ff // tf),
        in_specs=[pl.BlockSpec((bb, tl, d), lambda i, j, k: (i, j, 0), **x_mode),
                  pl.BlockSpec((1, d), lambda i, j, k: (0, 0)),
                  vec, vec, vec,
                  pl.BlockSpec((None, d, tf), lambda i, j, k: (layer, 0, k)),
                  pl.BlockSpec((None, tf, d), lambda i, j, k: (layer, k, 0))],
        out_specs=pl.BlockSpec((bb, tl, d), lambda i, j, k: (i, j, 0), **x_mode),
        out_shape=jax.ShapeDtypeStruct((b, l, d), F32),
        scratch_shapes=[pltpu.VMEM((bb * tl, d), BF16)],
        compiler_params=_params("arbitrary", "arbitrary", "arbitrary"),
        name="mlp",
    )(x, g.reshape(1, d), scale, shift, gate, w_up, w_down)


def _final_norm_kernel(x_ref, g_ref, o_ref):
    x = x_ref[...]
    o_ref[...] = x * lax.rsqrt(jnp.mean(x * x, axis=-1, keepdims=True) + NORM_EPS) * g_ref[...]


def _final_norm(x, g):
    b, l, d = x.shape
    tl = min(l, 512)
    return pl.pallas_call(
        _final_norm_kernel,
        grid=(b, l // tl),
        in_specs=[pl.BlockSpec((1, tl, d), lambda i, j: (i, j, 0)),
                  pl.BlockSpec((1, d), lambda i, j: (0, 0))],
        out_specs=pl.BlockSpec((1, tl, d), lambda i, j: (i, j, 0)),
        out_shape=jax.ShapeDtypeStruct((b, l, d), F32),
        compiler_params=_params("arbitrary", "arbitrary"),
        name="final_norm",
    )(x, g.reshape(1, d))


def _online_update(s, v, m_s, l_s, acc_s, exp=jnp.exp):
    m_prev = m_s[...]
    m_new = jnp.maximum(m_prev, jnp.max(s, axis=1, keepdims=True))
    alpha = exp(m_prev - m_new)
    p = exp(s - m_new)
    l_s[...] = alpha * l_s[...] + jnp.sum(p, axis=1, keepdims=True)
    acc_s[...] = alpha * acc_s[...] + _dot(p.astype(BF16), v)
    m_s[...] = m_new


def _lane_column(x, j):
    lane = lax.broadcasted_iota(jnp.int32, x.shape, 1)
    return jnp.sum(jnp.where(lane == j, x, 0.0), axis=1, keepdims=True)


def _topk_bias(gate, n_valid, k):
    lane = lax.broadcasted_iota(jnp.int32, gate.shape, 1).astype(F32)
    g = jnp.where(lane < n_valid, gate, NEG)
    bias = jnp.full(gate.shape, NEG, F32)
    for _ in range(k):
        mx = jnp.max(g, axis=1, keepdims=True)
        first = jnp.min(jnp.where(g == mx, lane, float(gate.shape[1])), axis=1, keepdims=True)
        pick = (lane == first) & (mx > 0.5 * NEG)
        bias = jnp.where(pick, 0.0, bias)
        g = jnp.where(lane == first, NEG, g)
    return bias


def _diff_lambda(lam, lam_init):
    return (jnp.exp(jnp.sum(lam[0:1] * lam[1:2], axis=1, keepdims=True))
            - jnp.exp(jnp.sum(lam[2:3] * lam[3:4], axis=1, keepdims=True)) + lam_init)


def _sub_norm(o, gsub, lam_init):
    return o * lax.rsqrt(jnp.mean(o * o, axis=-1, keepdims=True) + SUBLN_EPS) * gsub * (1.0 - lam_init)


def _win_prompt_kernel(*refs, nw):
    in_refs, (o_ref, kcat_s, vcat_s, og_s, lse_s) = refs[:5 * N_GROUPS], refs[5 * N_GROUPS:]
    t = pl.program_id(2)
    tile = o_ref.shape[0]
    scale = HEAD_DIM ** -0.5 * LOG2E
    r = lax.broadcasted_iota(jnp.int32, (nw, 2 * nw), 0)
    c = lax.broadcasted_iota(jnp.int32, (nw, 2 * nw), 1)
    dist = nw + r - c
    band = (dist >= 0) & (dist <= nw)

    for g, (window, dil) in enumerate(DIL_GROUPS):
        q_ref, kc_ref, kp_ref, vc_ref, vp_ref = in_refs[5 * g:5 * g + 5]
        kcat_s[0:window, :] = kp_ref[...]
        kcat_s[window:window + tile, :] = kc_ref[...]
        vcat_s[0:window, :] = vp_ref[...]
        vcat_s[window:window + tile, :] = vc_ref[...]

        def blocks(ug, carry, g=g, dil=dil, window=window, q_ref=q_ref):
            work = []
            for i in range(WIN_UNROLL):
                u = ug * WIN_UNROLL + i
                start = u % dil + (u // dil) * window
                if dil == 1:
                    rows_q = pl.ds(start, nw)
                    rows_k = pl.ds(start, 2 * nw)
                else:
                    rows_q = pl.ds(start, nw, stride=dil)
                    rows_k = pl.ds(start, 2 * nw, stride=dil)
                q = (q_ref[rows_q, :] * scale).astype(BF16)
                k = kcat_s[rows_k, :].astype(BF16)
                first_key = jnp.where((t > 0) | (u >= dil), 0, nw)
                s = _dot_nt(q, k) + jnp.where(band & (c >= first_key), 0.0, NEG)
                work.append((rows_q, rows_k, s))
            for rows_q, rows_k, s in work:
                v = vcat_s[rows_k, :].astype(BF16)
                m = jnp.max(s, axis=1, keepdims=True)
                p = jnp.exp2(s - m)
                l = jnp.sum(p, axis=1, keepdims=True)
                og_s[g, rows_q, :] = _dot(p.astype(BF16), v) * (1.0 / l)
                lse_s[g, rows_q, :] = jnp.broadcast_to(m + jnp.log2(l), (nw, LANES))
            return carry

        lax.fori_loop(0, tile // nw // WIN_UNROLL, blocks, 0)

    def merge(rows):
        lses = [lse_s[g, rows, :] for g in range(N_GROUPS)]
        m = jnp.maximum(jnp.maximum(lses[0], lses[1]), lses[2])
        es = [jnp.exp2(v - m) for v in lses]
        inv = 1.0 / (es[0] + es[1] + es[2])
        o_ref[rows, :] = sum(es[g] * og_s[g, rows, :] for g in range(N_GROUPS)) * inv

    _for_row_chunks(1, tile, merge)


def _win_prompt(qkv):
    b, _, nh, l, hd = qkv.shape
    tile = WIN_TILE
    nw = DIL_GROUPS[0][0] // DIL_GROUPS[0][1]
    specs = []
    for g, (window, dil) in enumerate(DIL_GROUPS):
        assert window // dil == nw and tile % window == 0

        def cur(t, g=g):
            return pl.BlockSpec((None, None, None, tile, hd), lambda bi, h, i: (bi, 3 * g + t, h, i, 0))

        def prev(t, g=g, per=tile // window, window=window):
            return pl.BlockSpec((None, None, None, window, hd),
                                lambda bi, h, i: (bi, 3 * g + t, h, jnp.maximum(i * per - 1, 0), 0))

        specs += [cur(0), cur(1), prev(1), cur(2), prev(2)]
    max_w = DIL_GROUPS[-1][0]
    return pl.pallas_call(
        functools.partial(_win_prompt_kernel, nw=nw),
        grid=(b, nh, l // tile),
        in_specs=specs,
        out_specs=pl.BlockSpec((None, None, tile, hd), lambda bi, h, i: (bi, h, i, 0)),
        out_shape=jax.ShapeDtypeStruct((b, nh, l, hd), F32),
        scratch_shapes=[pltpu.VMEM((max_w + tile, hd), F32), pltpu.VMEM((max_w + tile, hd), F32),
                        pltpu.VMEM((N_GROUPS, tile, hd), F32), pltpu.VMEM((N_GROUPS, tile, LANES), F32)],
        compiler_params=_params("arbitrary", "arbitrary", "arbitrary"),
        name="win_prompt",
    )(*([qkv] * (5 * N_GROUPS)))


def _flash_prompt_kernel(*refs, diff, lam_init, tq):
    if diff:
        lam_ref, gsub_ref, q_ref, k_ref, v_ref, o_ref, qs_s, m_s, l_s, acc_s = refs
    else:
        km_ref, q_ref, k_ref, v_ref, o_ref, qs_s, m_s, l_s, acc_s = refs
    qi = pl.program_id(2)
    rows = qs_s.shape[0]
    lane = lax.broadcasted_iota(jnp.int32, (tq, HEAD_DIM), 1)
    q = q_ref[...]
    if diff:
        qsc = q * (DIFF_HALF ** -0.5 * LOG2E)
        qs_s[...] = jnp.concatenate([jnp.where(lane < DIFF_HALF, qsc, 0.0),
                                     jnp.where(lane >= DIFF_HALF, qsc, 0.0)], axis=0).astype(BF16)
    else:
        qs_s[...] = (q * (HEAD_DIM ** -0.5 * LOG2E)).astype(BF16)
        own = (qi * tq + lax.broadcasted_iota(jnp.int32, (tq, 1), 0)) // MOBA_BLOCK
        sel_bias = _topk_bias(_dot_nt_3pass(q, km_ref[...]), own.astype(F32), MOBA_TOPK)
    m_s[...] = jnp.full(m_s.shape, NEG, F32)
    l_s[...] = jnp.zeros(l_s.shape, F32)
    acc_s[...] = jnp.zeros(acc_s.shape, F32)

    chunk = FLASH_CHUNK
    diag = qi // (chunk // tq)

    def selection_bias(start, n_keys):
        cols = [jnp.broadcast_to(_lane_column(sel_bias, start // MOBA_BLOCK + t), (tq, MOBA_BLOCK))
                for t in range(n_keys // MOBA_BLOCK)]
        return jnp.concatenate(cols, axis=1)

    def consume(start, n_keys, bias):
        rows_k = pl.ds(pl.multiple_of(start, n_keys), n_keys)
        k = k_ref[rows_k, :].astype(BF16)
        v = v_ref[rows_k, :].astype(BF16)
        sub = FLASH_ROWS
        scores = [_dot_nt(qs_s[r0:r0 + sub, :], k) for r0 in range(0, rows, sub)]
        for r0, s in zip(range(0, rows, sub), scores):
            rs = slice(r0, r0 + sub)
            if bias is not None:
                s = s + bias[r0 % tq:r0 % tq + sub]
            _online_update(s, v, m_s.at[rs, :], l_s.at[rs, :], acc_s.at[rs, :], exp=jnp.exp2)

    def past_chunk(ci, carry):
        consume(ci * chunk, chunk, None if diff else selection_bias(ci * chunk, chunk))
        return carry

    lax.fori_loop(0, diag, past_chunk, 0)

    def diagonal_part(start, n_keys):
        q_pos = qi * tq + lax.broadcasted_iota(jnp.int32, (tq, n_keys), 0)
        k_pos = start + lax.broadcasted_iota(jnp.int32, (tq, n_keys), 1)
        bias = jnp.where(k_pos <= q_pos, 0.0, NEG)
        if not diff:
            bias = jnp.where(k_pos // MOBA_BLOCK >= q_pos // MOBA_BLOCK, bias, selection_bias(start, n_keys))
        consume(start, n_keys, bias)

    diagonal_part(diag * chunk, chunk)

    o = acc_s[...] * (1.0 / l_s[...])
    if diff:
        o = o[:tq] - _diff_lambda(lam_ref[...], lam_init) * o[tq:]
        o = _sub_norm(o, gsub_ref[...], lam_init)
    o_ref[...] = o


def _flash_prompt(qkv, *, diff, lam=None, gsub=None, lam_init=0.0, kmean=None):
    b, _, nh, l, hd = qkv.shape
    tq = MOBA_BLOCK if diff else 2 * MOBA_BLOCK
    assert l % FLASH_CHUNK == 0 and FLASH_CHUNK % tq == 0
    rows = 2 * tq if diff else tq
    qkv_specs = [pl.BlockSpec((None, None, None, tq, hd), lambda bi, h, i: (bi, 0, h, i, 0)),
                 pl.BlockSpec((None, None, None, l, hd), lambda bi, h, i: (bi, 1, h, 0, 0)),
                 pl.BlockSpec((None, None, None, l, hd), lambda bi, h, i: (bi, 2, h, 0, 0))]
    if diff:
        extra = [lam, gsub.reshape(1, hd)]
        extra_specs = [pl.BlockSpec(lam.shape, lambda bi, h, i: (0, 0)),
                       pl.BlockSpec((1, hd), lambda bi, h, i: (0, 0))]
    else:
        extra = [kmean]
        extra_specs = [pl.BlockSpec((None, None, kmean.shape[2], hd), lambda bi, h, i: (bi, h, 0, 0))]
    return pl.pallas_call(
        functools.partial(_flash_prompt_kernel, diff=diff, lam_init=lam_init, tq=tq),
        grid=(b, nh, l // tq),
        in_specs=extra_specs + qkv_specs,
        out_specs=pl.BlockSpec((None, None, tq, hd), lambda bi, h, i: (bi, h, i, 0)),
        out_shape=jax.ShapeDtypeStruct((b, nh, l, hd), F32),
        scratch_shapes=[pltpu.VMEM((rows, hd), BF16), pltpu.VMEM((rows, 1), F32),
                        pltpu.VMEM((rows, 1), F32), pltpu.VMEM((rows, hd), F32)],
        compiler_params=_params("arbitrary", "arbitrary", "arbitrary"),
        name="flash_diff" if diff else "flash_moba",
    )(*extra, qkv, qkv, qkv)


def _block_mean_kernel(k_ref, o_ref):
    l, hd = k_ref.shape
    o_ref[...] = jnp.sum(k_ref[...].reshape(l // MOBA_BLOCK, MOBA_BLOCK, hd), axis=1) * (1.0 / MOBA_BLOCK)


def _block_mean_prompt(qkv):
    b, _, nh, l, hd = qkv.shape
    nb = l // MOBA_BLOCK
    return pl.pallas_call(
        _block_mean_kernel,
        grid=(b, nh),
        in_specs=[pl.BlockSpec((None, None, None, l, hd), lambda bi, h: (bi, 1, h, 0, 0))],
        out_specs=pl.BlockSpec((None, None, nb, hd), lambda bi, h: (bi, h, 0, 0)),
        out_shape=jax.ShapeDtypeStruct((b, nh, nb, hd), F32),
        compiler_params=_params("arbitrary", "arbitrary"),
        name="block_mean",
    )(qkv)


def _decode_kernel(*refs, mode, n_pg, paged, lam_init):
    refs = list(refs)
    if paged:
        refs.pop(0)
    q_ref = refs.pop(0)
    pg_refs = [refs.pop(0) for _ in range(n_pg)]
    kn_ref, vn_ref = refs.pop(0), refs.pop(0)
    bias_ref, bnew_ref = refs.pop(0), refs.pop(0)
    km_ref = refs.pop(0) if mode == "moba" else None
    if mode == "diff":
        lam_ref, gsub_ref = refs.pop(0), refs.pop(0)
    o_ref = refs.pop(0)
    lse_ref = refs.pop(0) if mode == "win" else None
    q_s, kf_s, vf_s, m_s, l_s, acc_s = refs[:6]
    sel_s = refs[6] if mode == "moba" else None

    s_id = pl.program_id(1)
    nh, n, hd = q_ref.shape
    rows = q_s.shape[0]
    half = rows // 2
    hpb = nh // 2
    tk = pg_refs[0].shape[0]
    scale = (DIFF_HALF if mode == "diff" else HEAD_DIM) ** -0.5

    @pl.when(s_id == 0)
    def _():
        lane = lax.broadcasted_iota(jnp.int32, (n, hd), 1)
        pieces = []
        for h in range(nh):
            qh = q_ref[h] * scale
            if mode == "diff":
                pieces += [jnp.where(lane < DIFF_HALF, qh, 0.0), jnp.where(lane >= DIFF_HALF, qh, 0.0)]
            else:
                pieces.append(qh)
        q_s[...] = jnp.concatenate(pieces, axis=0).astype(BF16)
        m_s[...] = jnp.full(m_s.shape, NEG, F32)
        l_s[...] = jnp.zeros(l_s.shape, F32)
        acc_s[...] = jnp.zeros(acc_s.shape, F32)
        if mode == "moba":
            n_blk = km_ref.shape[0] // nh
            for h in range(nh):
                gate = _dot_nt_3pass(q_ref[h], km_ref[pl.ds(h, n_blk, stride=nh), :])
                sel_s[h * n:(h + 1) * n, :] = _topk_bias(gate, n_blk, MOBA_TOPK)

    for hb in range(2):
        hs = slice(hb * hpb, (hb + 1) * hpb)
        rs = slice(hb * half, (hb + 1) * half)
        for r in range(n_pg):
            dst = slice(r * tk * hpb, (r + 1) * tk * hpb)
            kf_s[dst, :] = pg_refs[r][:, 0, hs, :].reshape(tk * hpb, hd).astype(BF16)
            vf_s[dst, :] = pg_refs[r][:, 1, hs, :].reshape(tk * hpb, hd).astype(BF16)
        s = _dot_nt(q_s[rs, :], kf_s[...]) + bias_ref[...]
        if mode == "moba":
            pages_per_blk = MOBA_BLOCK // tk
            cols = []
            for r in range(n_pg // pages_per_blk):
                col = _lane_column(sel_s[rs, :], s_id * (n_pg // pages_per_blk) + r)
                cols.append(jnp.broadcast_to(col, (half, MOBA_BLOCK * hpb)))
            s = s + jnp.concatenate(cols, axis=1)
        _online_update(s, vf_s[...], m_s.at[rs, :], l_s.at[rs, :], acc_s.at[rs, :])

    @pl.when(s_id == pl.num_programs(1) - 1)
    def _():
        pad = jnp.zeros((LANES - hpb * n, hd), F32)
        for hb in range(2):
            hs = slice(hb * hpb, (hb + 1) * hpb)
            rs = slice(hb * half, (hb + 1) * half)
            kn = jnp.concatenate([kn_ref[hs].reshape(hpb * n, hd), pad], axis=0).astype(BF16)
            vn = jnp.concatenate([vn_ref[hs].reshape(hpb * n, hd), pad], axis=0).astype(BF16)
            _online_update(_dot_nt(q_s[rs, :], kn) + bnew_ref[...], vn,
                           m_s.at[rs, :], l_s.at[rs, :], acc_s.at[rs, :])
        lsum = l_s[...]
        o = acc_s[...] * (1.0 / lsum)
        if mode == "diff":
            lam_full = _diff_lambda(lam_ref[...], lam_init)
            for h in range(nh):
                oh = o[2 * h * n:(2 * h + 1) * n] - lam_full * o[(2 * h + 1) * n:(2 * h + 2) * n]
                o_ref[h] = _sub_norm(oh, gsub_ref[...], lam_init)
        else:
            o_ref[...] = o.reshape(nh, n, hd)
        if mode == "win":
            lse_ref[...] = jnp.broadcast_to(m_s[...] + jnp.log(lsum), (rows, LANES)).reshape(nh, n, LANES)


def _head_match_bias(half_rows, rows_per_head, n_keys, hpb):
    r = np.arange(half_rows)[:, None] // rows_per_head
    c = np.arange(n_keys * hpb)[None, :] % hpb
    return np.where(r == c, 0.0, NEG)


def _decode_attn(qkv, gt, cache, layer, *, mode, page_table=None, dil=1, window=None,
                 kmean=None, lam=None, gsub=None, lam_init=0.0):
    b, _, nh, n, hd = qkv.shape
    hpb = nh // 2
    comps = 2 if mode == "diff" else 1
    rows = nh * comps * n
    half = rows // 2
    paged = page_table is not None
    tk = PAGE_SIZE
    total_pages = page_table.shape[1] if paged else cache.shape[2] // tk
    n_pg = min(PAGES_PER_STEP, total_pages)
    steps = total_pages // n_pg
    blk = (None, None, tk, 2, nh, hd)
    if paged:
        pg_specs = [pl.BlockSpec(blk, functools.partial(
            lambda bi, s, pt, r: (layer, pt[bi, s * n_pg + r], 0, 0, 0, 0), r=r)) for r in range(n_pg)]
    else:
        pg_specs = [pl.BlockSpec(blk, functools.partial(
            lambda bi, s, r: (layer, bi, s * n_pg + r, 0, 0, 0), r=r)) for r in range(n_pg)]

    def fixed(shape, idx):
        if paged:
            return pl.BlockSpec(shape, lambda bi, s, pt: idx(bi, s))
        return pl.BlockSpec(shape, lambda bi, s: idx(bi, s))

    qi = np.arange(half)[:, None] % n
    bias = _head_match_bias(half, comps * n, total_pages * tk if mode == "win" else n_pg * tk, hpb)
    cn = np.arange(LANES)[None, :]
    new_ok = (cn < hpb * n) & (cn // n == np.arange(half)[:, None] // (comps * n)) & (cn % n <= qi)
    if mode == "win":
        n_buf = total_pages * tk
        dist = n_buf + qi - np.arange(n_buf * hpb)[None, :] // hpb
        bias = np.where((dist % dil == 0) & (dist <= window), bias, NEG)
        new_ok &= (qi - cn % n) % dil == 0
    bias = jnp.asarray(bias, F32)
    bias_new = jnp.asarray(np.where(new_ok, 0.0, NEG), F32)

    head_blk = (None, None, nh, n, hd)
    args = [qkv] + [cache] * n_pg + [qkv, qkv, bias, bias_new]
    specs = [fixed(head_blk, lambda bi, s: (bi, gt, 0, 0, 0))] + pg_specs + [
        fixed(head_blk, lambda bi, s: (bi, gt + 1, 0, 0, 0)),
        fixed(head_blk, lambda bi, s: (bi, gt + 2, 0, 0, 0)),
        fixed((half, n_pg * tk * hpb), (lambda bi, s: (0, s)) if mode == "win" else (lambda bi, s: (0, 0))),
        fixed((half, LANES), lambda bi, s: (0, 0))]
    if mode == "moba":
        n_blk = kmean.shape[1]
        args.append(kmean.reshape(b, n_blk * nh, hd))
        specs.append(fixed((None, n_blk * nh, hd), lambda bi, s: (bi, 0, 0)))
    if mode == "diff":
        args += [lam, gsub.reshape(1, hd)]
        specs += [fixed(lam.shape, lambda bi, s: (0, 0)), fixed((1, hd), lambda bi, s: (0, 0))]
    scratch = [pltpu.VMEM((rows, hd), BF16),
               pltpu.VMEM((n_pg * tk * hpb, hd), BF16), pltpu.VMEM((n_pg * tk * hpb, hd), BF16),
               pltpu.VMEM((rows, 1), F32), pltpu.VMEM((rows, 1), F32), pltpu.VMEM((rows, hd), F32)]
    if mode == "moba":
        scratch.append(pltpu.VMEM((rows, n_blk), F32))
    out_blk = (None, nh, n, hd)
    out_shape = [jax.ShapeDtypeStruct((b, nh, n, hd), F32)]
    out_specs = [fixed(out_blk, lambda bi, s: (bi, 0, 0, 0))]
    if mode == "win":
        out_shape.append(jax.ShapeDtypeStruct((b, nh, n, LANES), F32))
        out_specs.append(fixed(out_blk, lambda bi, s: (bi, 0, 0, 0)))
    kern = functools.partial(_decode_kernel, mode=mode, n_pg=n_pg, paged=paged, lam_init=lam_init)
    name = "decode_" + mode
    if paged:
        grid_spec = pltpu.PrefetchScalarGridSpec(num_scalar_prefetch=1, grid=(b, steps), in_specs=specs,
                                                 out_specs=out_specs, scratch_shapes=scratch)
        out = pl.pallas_call(kern, grid_spec=grid_spec, out_shape=out_shape, name=name,
                             compiler_params=_params("arbitrary", "arbitrary"))(page_table, *args)
    else:
        out = pl.pallas_call(kern, grid=(b, steps), in_specs=specs, out_specs=out_specs, out_shape=out_shape,
                             scratch_shapes=scratch, name=name,
                             compiler_params=_params("arbitrary", "arbitrary"))(*args)
    return out if mode == "win" else out[0]


def _merge_groups_kernel(*refs):
    o_refs, lse_refs, out_ref = refs[:N_GROUPS], refs[N_GROUPS:2 * N_GROUPS], refs[2 * N_GROUPS]
    lses = [r[...] for r in lse_refs]
    m = jnp.maximum(jnp.maximum(lses[0], lses[1]), lses[2])
    es = [jnp.exp(v - m) for v in lses]
    inv = 1.0 / (es[0] + es[1] + es[2])
    out_ref[...] = sum(es[g] * o_refs[g][...] for g in range(N_GROUPS)) * inv


def _merge_groups(outs, lses):
    shape = outs[0].shape
    spec = pl.BlockSpec(shape, lambda i: (0,) * len(shape))
    return pl.pallas_call(
        _merge_groups_kernel, grid=(1,), in_specs=[spec] * (2 * N_GROUPS), out_specs=spec,
        out_shape=jax.ShapeDtypeStruct(shape, F32), compiler_params=_params("arbitrary"),
        name="merge_groups",
    )(*outs, *lses)


def _paged_mean_kernel(pt_ref, *refs, n_pages, pages_per_blk):
    o_ref = refs[n_pages]
    for j in range(n_pages // pages_per_blk):
        tot = sum(jnp.sum(refs[j * pages_per_blk + r][...], axis=0) for r in range(pages_per_blk))
        o_ref[j] = tot * (1.0 / MOBA_BLOCK)


def _paged_block_mean(cache, layer, page_table):
    b, total_pages = page_table.shape
    tk, _, nh, hd = cache.shape[2:]
    pages_per_blk = MOBA_BLOCK // tk
    n_pages = 8
    steps = total_pages // n_pages
    specs = [pl.BlockSpec((None, None, tk, None, nh, hd), functools.partial(
        lambda bi, s, pt, r: (layer, pt[bi, s * n_pages + r], 0, 0, 0, 0), r=r)) for r in range(n_pages)]
    return pl.pallas_call(
        functools.partial(_paged_mean_kernel, n_pages=n_pages, pages_per_blk=pages_per_blk),
        grid_spec=pltpu.PrefetchScalarGridSpec(
            num_scalar_prefetch=1, grid=(b, steps), in_specs=specs,
            out_specs=pl.BlockSpec((None, n_pages // pages_per_blk, nh, hd), lambda bi, s, pt: (bi, s, 0, 0))),
        out_shape=jax.ShapeDtypeStruct((b, total_pages // pages_per_blk, nh, hd), F32),
        compiler_params=_params("arbitrary", "arbitrary"),
        name="paged_block_mean",
    )(page_table, *([cache] * n_pages))


def _shift_window_kernel(cur_ref, nxt_ref, new_ref, o_ref):
    t = cur_ref.shape[0]
    n = new_ref.shape[0]
    o_ref[0:t - n] = cur_ref[n:t]
    last = pl.program_id(2) == pl.num_programs(2) - 1

    @pl.when(last)
    def _():
        o_ref[t - n:t] = new_ref[...]

    @pl.when(jnp.logical_not(last))
    def _():
        o_ref[t - n:t] = nxt_ref[...]


def _shift_window(cache, new):
    layers, b, w = cache.shape[:3]
    n = new.shape[2]
    row = cache.shape[3:]
    t = min(w, 512)
    steps = w // t
    last_head = w // n - 1
    return pl.pallas_call(
        _shift_window_kernel,
        grid=(layers, b, steps),
        in_specs=[pl.BlockSpec((None, None, t) + row, lambda j, bi, i: (j, bi, i, 0, 0, 0)),
                  pl.BlockSpec((None, None, n) + row,
                               lambda j, bi, i: (j, bi, jnp.minimum((i + 1) * (t // n), last_head), 0, 0, 0)),
                  pl.BlockSpec((None, None, n) + row, lambda j, bi, i: (j, bi, 0, 0, 0, 0))],
        out_specs=pl.BlockSpec((None, None, t) + row, lambda j, bi, i: (j, bi, i, 0, 0, 0)),
        out_shape=jax.ShapeDtypeStruct(cache.shape, cache.dtype),
        compiler_params=_params("arbitrary", "arbitrary", "arbitrary"),
        name="shift_window",
    )(cache, cache, new)


def _rows_kernel(a_ref, b_ref, o_ref):
    layer = pl.program_id(0)
    nh = a_ref.shape[0]

    @pl.when(layer == 0)
    def _():
        for h in range(nh):
            o_ref[:, h, :] = a_ref[h]

    @pl.when(layer == 1)
    def _():
        for h in range(nh):
            o_ref[:, h, :] = b_ref[h]


def _rows_of_two(kv_a, kv_b, first, n_rows):
    b, _, nh, seq, hd = kv_a.shape
    l = n_rows
    tl = min(512, l)
    nt = l // tl
    assert (seq - l) % tl == 0
    skip = (seq - l) // tl

    def src(which):
        def index(layer, bi, i, kv):
            live = layer == which
            return (jnp.where(live, bi, (1 - which) * (b - 1)), first + jnp.where(live, kv, (1 - which)), 0,
                    skip + jnp.where(live, i, (1 - which) * (nt - 1)), 0)
        return pl.BlockSpec((None, None, nh, tl, hd), index)

    return pl.pallas_call(
        _rows_kernel,
        grid=(2, b, nt, 2),
        in_specs=[src(0), src(1)],
        out_specs=pl.BlockSpec((None, None, tl, None, nh, hd), lambda layer, bi, i, kv: (layer, bi, i, kv, 0, 0)),
        out_shape=jax.ShapeDtypeStruct((2, b, l, 2, nh, hd), F32),
        compiler_params=_params("arbitrary", "arbitrary", "arbitrary", "arbitrary"),
        name="rows_of",
    )(kv_a, kv_b)


def _rope_tables(pos, half, reps=1):
    inv = ROPE_THETA ** (-np.arange(0, 2 * half, 2, dtype=np.float64) / (2 * half))
    ang = np.asarray(pos, np.float64)[:, None] * inv[None, :]
    cos = np.concatenate([np.cos(ang), np.cos(ang)], axis=1)
    sin = np.concatenate([-np.sin(ang), np.sin(ang)], axis=1)
    tile = LANES // (2 * half)
    cos = np.tile(np.tile(cos, (1, tile)), (reps, 1))
    sin = np.tile(np.tile(sin, (1, tile)), (reps, 1))
    return (jnp.asarray(np.stack([cos, np.ones_like(cos)]), F32),
            jnp.asarray(np.stack([sin, np.zeros_like(sin)]), F32))


def _rows_of(kv_layers):
    return jnp.transpose(jnp.stack(kv_layers), (0, 1, 4, 2, 3, 5))


def kernel(x_prompt, x_sample, c_prompt, c_sample, cache_win1_kv, cache_win2_kv, cache_win3_kv, cache_kv,
           page_table, norm1_g, norm2_g, w_mod, b_mod, w_qkv_a, w_o_a, w_qkv_b, w_o_b, lambda_b, subln_g_b,
           w_qkv_c, w_o_c, w_up, w_down, final_g):
    bp, seq, d = x_prompt.shape
    bs, n_new, _ = x_sample.shape
    depth = w_mod.shape[0]
    past_len = page_table.shape[1] * PAGE_SIZE
    assert past_len % MOBA_BLOCK == 0 and seq % WIN_TILE == 0 and n_new == SUBLANES
    assert cache_kv.shape[2] == PAGE_SIZE
    pos_p = np.arange(seq)
    pos_s = past_len + np.arange(n_new)
    tl_p = 1024

    rope_p = {h: _rope_tables(pos_p, h) for h in (HEAD_DIM // 2, DIFF_HALF // 2)}
    rope_s = {h: _rope_tables(pos_s, h, reps=bs) for h in (HEAD_DIM // 2, DIFF_HALF // 2)}

    rows_c = 2 * SUBLANES
    c_all = jnp.concatenate([c_prompt, c_sample, jnp.zeros((rows_c - bp - bs, d), F32)], axis=0)
    mods = _modulation(c_all, w_mod, b_mod)
    win_caches = (cache_win1_kv, cache_win2_kv, cache_win3_kv)

    xp, xs = x_prompt, x_sample
    win_p = [[] for _ in range(N_GROUPS)]
    win_s = [[] for _ in range(N_GROUPS)]
    kv_p, kv_s = [], []
    for i in range(depth):
        kind = i % N_MIXERS
        j = i // N_MIXERS
        mp = [mods[i, :bp, t * d:(t + 1) * d][:, None, :] for t in range(6)]
        ms = [mods[i, bp:bp + bs, t * d:(t + 1) * d][:, None, :] for t in range(6)]
        if kind == 0:
            half = HEAD_DIM // 2
            qkv_p = _qkv(xp, norm1_g[i], mp[1], mp[0], *rope_p[half], w_qkv_a, j, half=half, tl=tl_p)
            qkv_s = _qkv(xs, norm1_g[i], ms[1], ms[0], *rope_s[half], w_qkv_a, j, half=half, tl=n_new)
            o_p = _win_prompt(qkv_p)
            outs_s, lses_s = [], []
            for g, (window, dil) in enumerate(DIL_GROUPS):
                win_p[g].append(qkv_p)
                o_g, lse_g = _decode_attn(qkv_s, 3 * g, win_caches[g], j, mode="win", dil=dil, window=window)
                outs_s.append(o_g)
                lses_s.append(lse_g)
                win_s[g].append(qkv_s[:, 3 * g + 1:3 * g + 3])
            o_s = _merge_groups(outs_s, lses_s)
            w_o = w_o_a
        else:
            slot = i - (i + 2) // N_MIXERS
            if kind == 1:
                half = DIFF_HALF // 2
                w_qkv, w_o = w_qkv_b, w_o_b
            else:
                half = HEAD_DIM // 2
                w_qkv, w_o = w_qkv_c, w_o_c
            qkv_p = _qkv(xp, norm1_g[i], mp[1], mp[0], *rope_p[half], w_qkv, j, half=half, tl=tl_p)
            qkv_s = _qkv(xs, norm1_g[i], ms[1], ms[0], *rope_s[half], w_qkv, j, half=half, tl=n_new)
            if kind == 1:
                lam_init = 0.8 - 0.6 * math.exp(-0.3 * i)
                o_p = _flash_prompt(qkv_p, diff=True, lam=lambda_b[j], gsub=subln_g_b[j], lam_init=lam_init)
                o_s = _decode_attn(qkv_s, 0, cache_kv, slot, mode="diff", page_table=page_table,
                                   lam=lambda_b[j], gsub=subln_g_b[j], lam_init=lam_init)
            else:
                o_p = _flash_prompt(qkv_p, diff=False, kmean=_block_mean_prompt(qkv_p))
                o_s = _decode_attn(qkv_s, 0, cache_kv, slot, mode="moba", page_table=page_table,
                                   kmean=_paged_block_mean(cache_kv, slot, page_table))
            kv_p.append(qkv_p)
            kv_s.append(qkv_s[:, 1:3])
        xp = _proj(o_p, w_o, j, mp[2], xp, tl=tl_p)
        xs = _proj(o_s, w_o, j, ms[2], xs, tl=n_new)
        xp = _mlp(xp, norm2_g[i], mp[4], mp[3], mp[5], w_up, w_down, i, tl=tl_p)
        xs = _mlp(xs, norm2_g[i], ms[4], ms[3], ms[5], w_up, w_down, i, tl=n_new)
    y_prompt = _final_norm(xp, final_g)
    y_sample = _final_norm(xs, final_g)
    new_win_s = [_shift_window(win_caches[g], _rows_of(win_s[g])) for g in range(N_GROUPS)]
    def prompt_rows(layers, first, keep):
        if len(layers) == 2:
            return _rows_of_two(layers[0], layers[1], first, keep)
        return _rows_of([t[:, first:first + 2, :, seq - keep:] for t in layers])

    new_win_p = [prompt_rows(win_p[g], 3 * g + 1, min(DIL_GROUPS[g][0], seq)) for g in range(N_GROUPS)]
    new_kv_p = prompt_rows(kv_p, 1, seq)
    return (y_prompt, y_sample,
            new_win_p[0], new_win_p[1], new_win_p[2], new_kv_p,
            new_win_s[0], new_win_s[1], new_win_s[2], _rows_of(kv_s))
```
